```python
import jax, jax.numpy as jnp
from jax import lax
import numpy as np

D_MODEL = 4096
BATCH = 4
SEQ = 2048
DEPTH = 4
DEC_BATCH = 8
DEC_SEQ = 16
PAST_LEN = 2048

CHUNK = 64
ATT_HEADS = 16
ATT_HEAD_DIM = 128
ATT_WIDTH = ATT_HEADS * ATT_HEAD_DIM
ATT_LEFT_CHUNKS = 8
ATT_BAND = ATT_LEFT_CHUNKS * CHUNK
REL_CLIP = 128
N_REL = 2 * REL_CLIP + 1
RET_HEADS = 8
RET_KEY_DIM = 256
RET_VAL_DIM = 512
RET_QK_WIDTH = RET_HEADS * RET_KEY_DIM
RET_V_WIDTH = RET_HEADS * RET_VAL_DIM
ROPE_BASE = 10000.0
CONV_CHANNELS = 2048
CONV_WIDTH = 31
FFN_HIDDEN = -(-8 * D_MODEL // (3 * 256)) * 256
IN_SIZES = (ATT_WIDTH, ATT_WIDTH, ATT_WIDTH,
            RET_QK_WIDTH, RET_QK_WIDTH, RET_V_WIDTH, RET_V_WIDTH,
            CONV_CHANNELS, CONV_CHANNELS,
            D_MODEL, D_MODEL, D_MODEL)
N_IN = 3 * ATT_WIDTH + 2 * RET_QK_WIDTH + 2 * RET_V_WIDTH + 2 * CONV_CHANNELS + 3 * D_MODEL
NEG_INF = -1e30

kernel_name = 'hybrid_streaming_encoder_step'


def rms_norm(x, w, eps=1e-6):
    xf = x.astype(jnp.float32)
    y = xf * lax.rsqrt(jnp.mean(xf * xf, axis=-1, keepdims=True) + eps)
    return (y * w.astype(jnp.float32)).astype(x.dtype)


def split_in(h):
    offs, acc = [], 0
    for s in IN_SIZES[:-1]:
        acc += s
        offs.append(acc)
    return jnp.split(h, offs, axis=-1)


def rel_bias(table, q_len, k_len, k_offset):
    d = jnp.arange(q_len)[:, None] + k_offset - jnp.arange(k_len)[None, :]
    idx = jnp.clip(d, -REL_CLIP, REL_CLIP) + REL_CLIP
    return jnp.transpose(table[idx], (2, 0, 1)).astype(jnp.float32)


def chunk_band_attention(q, k, v, table):
    B, T, H, Dh = q.shape
    n = T // CHUNK
    nb = ATT_LEFT_CHUNKS + 1
    q = q.reshape(B, n, CHUNK, H, Dh)
    pad = ((0, 0), (ATT_LEFT_CHUNKS, 0), (0, 0), (0, 0), (0, 0))
    kp = jnp.pad(k.reshape(B, n, CHUNK, H, Dh), pad)
    vp = jnp.pad(v.reshape(B, n, CHUNK, H, Dh), pad)
    band = jnp.arange(n)[:, None] + jnp.arange(nb)[None, :]
    kb = kp[:, band].reshape(B, n, nb * CHUNK, H, Dh)
    vb = vp[:, band].reshape(B, n, nb * CHUNK, H, Dh)
    s = jnp.einsum('bnqhd,bnkhd->bnhqk', q, kb, preferred_element_type=jnp.float32) * (Dh ** -0.5)
    s = s + rel_bias(table, CHUNK, nb * CHUNK, ATT_BAND)
    valid = jnp.repeat(band >= ATT_LEFT_CHUNKS, CHUNK, axis=1)
    s = jnp.where(valid[None, :, None, None, :], s, NEG_INF)
    p = jax.nn.softmax(s, axis=-1).astype(v.dtype)
    o = jnp.einsum('bnhqk,bnkhd->bnqhd', p, vb)
    return o.reshape(B, T, H * Dh)


def cached_band_attention(q, k, v, ck, cv, table):
    B, L, H, Dh = q.shape
    W = ck.shape[1]
    kk = jnp.concatenate([ck.astype(k.dtype), k], axis=1)
    vv = jnp.concatenate([cv.astype(v.dtype), v], axis=1)
    s = jnp.einsum('bqhd,bkhd->bhqk', q, kk, preferred_element_type=jnp.float32) * (Dh ** -0.5)
    s = s + rel_bias(table, L, W + L, W)
    p = jax.nn.softmax(s, axis=-1).astype(vv.dtype)
    return jnp.einsum('bhqk,bkhd->bqhd', p, vv).reshape(B, L, H * Dh)


def rotary(x, pos):
    half = x.shape[-1] // 2
    inv = 1.0 / (ROPE_BASE ** (jnp.arange(half, dtype=jnp.float32) / half))
    ang = pos.astype(jnp.float32)[:, None] * inv[None, :]
    cos, sin = jnp.cos(ang)[None, :, None, :], jnp.sin(ang)[None, :, None, :]
    xf = x.astype(jnp.float32)
    x1, x2 = xf[..., :half], xf[..., half:]
    return jnp.concatenate([x1 * cos - x2 * sin, x1 * sin + x2 * cos], axis=-1).astype(x.dtype)


def retention_block(s, q, k, v):
    L = q.shape[1]
    lg = jnp.log1p(-jnp.exp2(-5.0 - jnp.arange(RET_HEADS, dtype=jnp.float32)))
    i = jnp.arange(L, dtype=jnp.float32)
    diff = i[:, None] - i[None, :]
    decay = jnp.where(diff >= 0, jnp.exp(lg[:, None, None] * jnp.maximum(diff, 0.0)), 0.0)
    q_decay = jnp.exp(lg[None, :] * (i[:, None] + 1.0))
    k_decay = jnp.exp(lg[None, :] * (L - 1.0 - i)[:, None])
    qf, kf, vf = q.astype(jnp.float32), k.astype(jnp.float32), v.astype(jnp.float32)
    scores = jnp.einsum('bihd,bjhd->bhij', qf, kf) * decay
    o = (jnp.einsum('bhij,bjhe->bihe', scores, vf)
         + jnp.einsum('bihd,bhde->bihe', qf, s) * q_decay[None, :, :, None])
    s_new = (jnp.exp(lg * L)[None, :, None, None] * s
             + jnp.einsum('bjhd,bjhe->bhde', kf * k_decay[None, :, :, None], vf))
    return s_new, o


def multiscale_retention(s0, q, k, v):
    B, T = q.shape[:2]
    blk = min(T, CHUNK)
    n = T // blk

    def to_blocks(t):
        return t.reshape(B, n, blk, RET_HEADS, t.shape[-1]).swapaxes(0, 1)

    s_fin, o = lax.scan(lambda s, xs: retention_block(s, *xs), s0,
                        (to_blocks(q), to_blocks(k), to_blocks(v)))
    return s_fin, o.swapaxes(0, 1).reshape(B, T, RET_HEADS, RET_VAL_DIM)


def retention_output(o, gn_w, g):
    B, T = o.shape[:2]
    of = o.astype(jnp.float32)
    mu = jnp.mean(of, axis=-1, keepdims=True)
    var = jnp.mean(jnp.square(of - mu), axis=-1, keepdims=True)
    y = ((of - mu) * lax.rsqrt(var + 1e-5)).reshape(B, T, RET_V_WIDTH) * gn_w.astype(jnp.float32)
    return (y * jax.nn.silu(g.astype(jnp.float32))).astype(g.dtype)


def conv_tail(cat, dw_w, dw_b, ln_w, ln_b):
    y = lax.conv_general_dilated(cat, dw_w[:, None, :].astype(cat.dtype), window_strides=(1,),
                                 padding='VALID', dimension_numbers=('NWC', 'WIO', 'NWC'),
                                 feature_group_count=CONV_CHANNELS)
    y = y.astype(jnp.float32) + dw_b.astype(jnp.float32)
    mu = jnp.mean(y, axis=-1, keepdims=True)
    var = jnp.mean(jnp.square(y - mu), axis=-1, keepdims=True)
    y = (y - mu) * lax.rsqrt(var + 1e-5) * ln_w.astype(jnp.float32) + ln_b.astype(jnp.float32)
    return jax.nn.silu(y).astype(cat.dtype)


def token_mixers(xn, pos, ck, cv, s0, conv_prev, w_in, rel_table, gn_w, dw_w, dw_b, ln_w, ln_b,
                 w_ba, w_br, w_bc, w_o):
    B, T, _ = xn.shape
    h = jnp.einsum('btd,dn->btn', xn, w_in)
    qa, ka, va, qr, kr, vr, gr, ca, cb, ga, gb, gc = split_in(h)

    def heads(t, n):
        return t.reshape(B, T, n, t.shape[-1] // n)

    qa, ka, va = heads(qa, ATT_HEADS), heads(ka, ATT_HEADS), heads(va, ATT_HEADS)
    if ck is None:
        ya = chunk_band_attention(qa, ka, va, rel_table)
        keep = min(ATT_BAND, T)
        new_k, new_v = ka[:, T - keep:], va[:, T - keep:]
    else:
        ya = cached_band_attention(qa, ka, va, ck, cv, rel_table)
        new_k, new_v = ka, va
    qr = rotary(heads(qr, RET_HEADS), pos)
    kr = rotary(heads(kr, RET_HEADS), pos) * (RET_KEY_DIM ** -0.5)
    s_new, o = multiscale_retention(s0, qr, kr, heads(vr, RET_HEADS))
    yr = retention_output(o, gn_w, gr)
    u = ca * jax.nn.sigmoid(cb)
    cat = jnp.concatenate([conv_prev.astype(u.dtype), u], axis=1)
    yc = conv_tail(cat, dw_w, dw_b, ln_w, ln_b)
    new_conv = cat[:, -(CONV_WIDTH - 1):]
    merged = (jax.nn.sigmoid(ga) * jnp.einsum('btc,cd->btd', ya, w_ba)
              + jax.nn.sigmoid(gb) * jnp.einsum('btc,cd->btd', yr, w_br)
              + jax.nn.sigmoid(gc) * jnp.einsum('btc,cd->btd', yc, w_bc))
    out = jnp.einsum('btd,de->bte', merged, w_o)
    return out, new_k, new_v, s_new, new_conv


def swiglu(x, wg, wu, wd):
    a = jnp.einsum('btd,df->btf', x, wg)
    b = jnp.einsum('btd,df->btf', x, wu)
    return jnp.einsum('btf,fd->btd', jax.nn.silu(a) * b, wd)


def encoder_layer(x, pos, ck, cv, s0, conv_prev, norms, mix_w, ffn_w):
    n_mp, n_mq, n_fp, n_fq = norms
    mix, nk, nv, ns, nc = token_mixers(rms_norm(x, n_mp), pos, ck, cv, s0, conv_prev, *mix_w)
    x = x + rms_norm(mix, n_mq)
    x = x + rms_norm(swiglu(rms_norm(x, n_fp), *ffn_w), n_fq)
    return x, nk, nv, ns, nc


def setup_inputs(seed: int = 0) -> dict:
    key = jax.random.key(seed)
    keys = jax.random.split(key, 24)

    def nrm(i, shape, scale):
        return jax.random.normal(keys[i], shape, jnp.float32) * scale

    att_keep = min(ATT_BAND, PAST_LEN)
    return {
        'x_prompt': nrm(0, (BATCH, SEQ, D_MODEL), 1.0),
        'x_sample': nrm(1, (DEC_BATCH, DEC_SEQ, D_MODEL), 1.0),
        'cache_attn_k': nrm(2, (DEPTH, DEC_BATCH, att_keep, ATT_HEADS, ATT_HEAD_DIM), 1.0),
        'cache_attn_v': nrm(3, (DEPTH, DEC_BATCH, att_keep, ATT_HEADS, ATT_HEAD_DIM), 1.0),
        'state_ret': nrm(4, (DEPTH, DEC_BATCH, RET_HEADS, RET_KEY_DIM, RET_VAL_DIM), 0.05),
        'cache_conv': nrm(5, (DEPTH, DEC_BATCH, CONV_WIDTH - 1, CONV_CHANNELS), 0.5),
        'norm_mix_pre': 1.0 + nrm(6, (DEPTH, D_MODEL), 0.05),
        'norm_mix_post': 1.0 + nrm(7, (DEPTH, D_MODEL), 0.05),
        'norm_ffn_pre': 1.0 + nrm(8, (DEPTH, D_MODEL), 0.05),
        'norm_ffn_post': 1.0 + nrm(9, (DEPTH, D_MODEL), 0.05),
        'w_in': nrm(10, (DEPTH, D_MODEL, N_IN), D_MODEL ** -0.5),
        'attn_rel_bias': nrm(11, (DEPTH, N_REL, ATT_HEADS), 0.1),
        'ret_gn_w': 1.0 + nrm(12, (DEPTH, RET_V_WIDTH), 0.05),
        'conv_dw_w': nrm(13, (DEPTH, CONV_WIDTH, CONV_CHANNELS), CONV_WIDTH ** -0.5),
        'conv_dw_b': nrm(14, (DEPTH, CONV_CHANNELS), 0.02),
        'conv_ln_w': 1.0 + nrm(15, (DEPTH, CONV_CHANNELS), 0.05),
        'conv_ln_b': nrm(16, (DEPTH, CONV_CHANNELS), 0.02),
        'w_branch_attn': nrm(17, (DEPTH, ATT_WIDTH, D_MODEL), ATT_WIDTH ** -0.5),
        'w_branch_ret': nrm(18, (DEPTH, RET_V_WIDTH, D_MODEL), RET_V_WIDTH ** -0.5),
        'w_branch_conv': nrm(19, (DEPTH, CONV_CHANNELS, D_MODEL), CONV_CHANNELS ** -0.5),
        'w_out': nrm(20, (DEPTH, D_MODEL, D_MODEL), D_MODEL ** -0.5),
        'w_ffn_gate': nrm(21, (DEPTH, D_MODEL, FFN_HIDDEN), D_MODEL ** -0.5),
        'w_ffn_up': nrm(22, (DEPTH, D_MODEL, FFN_HIDDEN), D_MODEL ** -0.5),
        'w_ffn_down': nrm(23, (DEPTH, FFN_HIDDEN, D_MODEL), FFN_HIDDEN ** -0.5),
    }


def reference(x_prompt, x_sample, cache_attn_k, cache_attn_v, state_ret, cache_conv,
              norm_mix_pre, norm_mix_post, norm_ffn_pre, norm_ffn_post, w_in, attn_rel_bias,
              ret_gn_w, conv_dw_w, conv_dw_b, conv_ln_w, conv_ln_b, w_branch_attn, w_branch_ret,
              w_branch_conv, w_out, w_ffn_gate, w_ffn_up, w_ffn_down):
    bp, t_p = x_prompt.shape[:2]
    t_s = x_sample.shape[1]
    pos_p = jnp.arange(t_p)
    pos_s = PAST_LEN + jnp.arange(t_s)
    s0_p = jnp.zeros((bp, RET_HEADS, RET_KEY_DIM, RET_VAL_DIM), jnp.float32)
    conv0_p = jnp.zeros((bp, CONV_WIDTH - 1, CONV_CHANNELS), x_prompt.dtype)
    xp, xs = x_prompt, x_sample
    kp_l, vp_l, sp_l, cp_l = [], [], [], []
    ks_l, vs_l, ss_l, cs_l = [], [], [], []
    for l in range(DEPTH):
        norms = (norm_mix_pre[l], norm_mix_post[l], norm_ffn_pre[l], norm_ffn_post[l])
        mix_w = (w_in[l], attn_rel_bias[l], ret_gn_w[l], conv_dw_w[l], conv_dw_b[l], conv_ln_w[l],
                 conv_ln_b[l], w_branch_attn[l], w_branch_ret[l], w_branch_conv[l], w_out[l])
        ffn_w = (w_ffn_gate[l], w_ffn_up[l], w_ffn_down[l])
        xp, nk, nv, ns, nc = encoder_layer(xp, pos_p, None, None, s0_p, conv0_p, norms, mix_w, ffn_w)
        kp_l.append(nk); vp_l.append(nv); sp_l.append(ns); cp_l.append(nc)
        xs, nk, nv, ns, nc = encoder_layer(xs, pos_s, cache_attn_k[l], cache_attn_v[l],
                                           state_ret[l].astype(jnp.float32), cache_conv[l],
                                           norms, mix_w, ffn_w)
        ks_l.append(nk); vs_l.append(nv); ss_l.append(ns); cs_l.append(nc)
    return (xp, xs,
            jnp.stack(kp_l), jnp.stack(vp_l), jnp.stack(sp_l), jnp.stack(cp_l),
            jnp.stack(ks_l), jnp.stack(vs_l), jnp.stack(ss_l), jnp.stack(cs_l))
```

```python
import functools

import jax
import jax.numpy as jnp
from jax import lax
from jax.experimental import pallas as pl
from jax.experimental.pallas import tpu as pltpu

F32 = jnp.float32
BF16 = jnp.bfloat16

D_MODEL = 4096
DEPTH = 4
BP, TP = 4, 2048
BS, TS = 8, 16
PAST_LEN = 2048
MP = BP * TP
M_ALL = MP + BS * TS

CHUNK = 64
ATT_HEADS, ATT_DH = 16, 128
ATT_WIDTH = ATT_HEADS * ATT_DH
ATT_BAND = 8 * CHUNK
REL_CLIP = 128
RET_HEADS, RET_DK, RET_DV = 8, 256, 512
RET_V_WIDTH = RET_HEADS * RET_DV
ROPE_BASE = 10000.0
CONV_C, CONV_W = 2048, 31
FFN_HIDDEN = 11008
NEG_INF = -1e30

QA, KA, VA = 0, 2048, 4096
QR, KR, VR, GR = 6144, 8192, 10240, 14336
CA, CB = 18432, 20480
GA, GB, GC = 22528, 26624, 30720
N_IN = 34816

VMEM_LIMIT_BYTES = 56 * 1024 * 1024

TM = 832
RET_L = 256
ATT_G = 256
ATT_KW = ATT_G + ATT_BAND
SAMPLE_KW = 640
SAMPLE_L = 128
CONV_TT = 64
CONV_HIST = 32


def _params(n_axes):
    return pltpu.CompilerParams(dimension_semantics=("arbitrary",) * n_axes,
                                vmem_limit_bytes=VMEM_LIMIT_BYTES)


def _sigmoid(x):
    return 1.0 / (1.0 + jnp.exp(-x))


def _silu(x):
    return x * _sigmoid(x)


def _cast_weight(w_ref, wb_ref):
    def body(c, carry):
        r = pl.multiple_of(c * 128, 128)
        wb_ref[pl.ds(r, 128), :] = w_ref[pl.ds(r, 128), :].astype(BF16)
        return carry
    lax.fori_loop(0, w_ref.shape[0] // 128, body, 0)


def _proj_kernel(*refs, has_addend):
    if has_addend:
        x_ref, w_ref, a_ref, o_ref, wb_ref = refs
    else:
        x_ref, w_ref, o_ref, wb_ref = refs

    @pl.when(pl.program_id(1) == 0)
    def _():
        _cast_weight(w_ref, wb_ref)

    acc = jnp.dot(x_ref[...], wb_ref[...], preferred_element_type=F32)
    if has_addend:
        acc = acc + a_ref[...]
    o_ref[...] = acc.astype(o_ref.dtype)


def _proj(x, w, layer, *, k_size, k_block, tn, out_dtype, addend=None, name):
    m = x.shape[0]
    n = w.shape[2]
    in_specs = [pl.BlockSpec((TM, k_size), lambda j, i: (i, k_block)),
                pl.BlockSpec((None, k_size, tn), lambda j, i: (layer, k_block, j))]
    args = [x, w]
    if addend is not None:
        in_specs.append(pl.BlockSpec((TM, tn), lambda j, i: (i, j)))
        args.append(addend)
    return pl.pallas_call(
        functools.partial(_proj_kernel, has_addend=addend is not None),
        out_shape=jax.ShapeDtypeStruct((m, n), out_dtype),
        grid=(n // tn, m // TM),
        in_specs=in_specs,
        out_specs=pl.BlockSpec((TM, tn), lambda j, i: (i, j)),
        scratch_shapes=[pltpu.VMEM((k_size, tn), BF16)],
        compiler_params=_params(2),
        name=name,
    )(*args)


def _swiglu_kernel(x_ref, wg_ref, wu_ref, o_ref, wgb_ref, wub_ref):
    @pl.when(pl.program_id(1) == 0)
    def _():
        _cast_weight(wg_ref, wgb_ref)
        _cast_weight(wu_ref, wub_ref)

    x = x_ref[...]
    a = jnp.dot(x, wgb_ref[...], preferred_element_type=F32)
    b = jnp.dot(x, wub_ref[...], preferred_element_type=F32)
    o_ref[...] = (_silu(a) * b).astype(o_ref.dtype)


def _swiglu_up(xn, w_gate, w_up, layer):
    tn = 256
    m = xn.shape[0]
    wspec = pl.BlockSpec((None, D_MODEL, tn), lambda j, i: (layer, 0, j))
    return pl.pallas_call(
        _swiglu_kernel,
        out_shape=jax.ShapeDtypeStruct((m, FFN_HIDDEN), BF16),
        grid=(FFN_HIDDEN // tn, m // TM),
        in_specs=[pl.BlockSpec((TM, D_MODEL), lambda j, i: (i, 0)), wspec, wspec],
        out_specs=pl.BlockSpec((TM, tn), lambda j, i: (i, j)),
        scratch_shapes=[pltpu.VMEM((D_MODEL, tn), BF16), pltpu.VMEM((D_MODEL, tn), BF16)],
        compiler_params=_params(2),
        name="ffn_gate_up",
    )(xn, w_gate, w_up)


def _merge_kernel(ya_ref, yr_ref, yc_ref, ga_ref, gb_ref, gc_ref, wa_ref, wr_ref, wc_ref,
                  o_ref, wab_ref, wrb_ref, wcb_ref):
    @pl.when(pl.program_id(1) == 0)
    def _():
        _cast_weight(wa_ref, wab_ref)
        _cast_weight(wr_ref, wrb_ref)
        _cast_weight(wc_ref, wcb_ref)

    a = jnp.dot(ya_ref[...], wab_ref[...], preferred_element_type=F32)
    r = jnp.dot(yr_ref[...], wrb_ref[...], preferred_element_type=F32)
    c = jnp.dot(yc_ref[...], wcb_ref[...], preferred_element_type=F32)
    out = (_sigmoid(ga_ref[...].astype(F32)) * a + _sigmoid(gb_ref[...].astype(F32)) * r
           + _sigmoid(gc_ref[...].astype(F32)) * c)
    o_ref[...] = out.astype(o_ref.dtype)


def _merge(ya, yr, yc, h, w_ba, w_br, w_bc, layer):
    tn = 256
    tm = 416
    m = ya.shape[0]

    def gate_spec(off):
        return pl.BlockSpec((tm, tn), lambda j, i: (i, off // tn + j))

    def w_spec(k):
        return pl.BlockSpec((None, k, tn), lambda j, i: (layer, 0, j))

    def x_spec(k):
        return pl.BlockSpec((tm, k), lambda j, i: (i, 0))

    return pl.pallas_call(
        _merge_kernel,
        out_shape=jax.ShapeDtypeStruct((m, D_MODEL), BF16),
        grid=(D_MODEL // tn, m // tm),
        in_specs=[x_spec(ATT_WIDTH), x_spec(RET_V_WIDTH), x_spec(CONV_C),
                  gate_spec(GA), gate_spec(GB), gate_spec(GC),
                  w_spec(ATT_WIDTH), w_spec(RET_V_WIDTH), w_spec(CONV_C)],
        out_specs=pl.BlockSpec((tm, tn), lambda j, i: (i, j)),
        scratch_shapes=[pltpu.VMEM((ATT_WIDTH, tn), BF16), pltpu.VMEM((RET_V_WIDTH, tn), BF16),
                        pltpu.VMEM((CONV_C, tn), BF16)],
        compiler_params=_params(2),
        name="branch_merge",
    )(ya, yr, yc, h, h, h, w_ba, w_br, w_bc)


def _rms(x, w, eps=1e-6):
    return x * lax.rsqrt(jnp.mean(x * x, axis=-1, keepdims=True) + eps) * w


def _pre_norm_kernel(x_ref, w_ref, o_ref):
    o_ref[...] = _rms(x_ref[...], w_ref[...]).astype(o_ref.dtype)


def _pre_norm(x, w, layer):
    tm = 208
    m = x.shape[0]
    return pl.pallas_call(
        _pre_norm_kernel,
        out_shape=jax.ShapeDtypeStruct((m, D_MODEL), BF16),
        grid=(m // tm,),
        in_specs=[pl.BlockSpec((tm, D_MODEL), lambda i: (i, 0)),
                  pl.BlockSpec((None, 1, D_MODEL), lambda i: (layer, 0, 0))],
        out_specs=pl.BlockSpec((tm, D_MODEL), lambda i: (i, 0)),
        compiler_params=_params(1),
        name="pre_norm",
    )(x, w)


def _post_norm_kernel(y_ref, x_ref, wp_ref, wn_ref, xo_ref, xn_ref):
    x_new = x_ref[...] + _rms(y_ref[...].astype(F32), wp_ref[...])
    xo_ref[...] = x_new
    xn_ref[...] = _rms(x_new, wn_ref[...]).astype(xn_ref.dtype)


def _post_norm(y, x, w_post, layer, w_next, next_layer):
    tm = 208
    m = x.shape[0]
    row = pl.BlockSpec((tm, D_MODEL), lambda i: (i, 0))
    return pl.pallas_call(
        _post_norm_kernel,
        out_shape=(jax.ShapeDtypeStruct((m, D_MODEL), F32),
                   jax.ShapeDtypeStruct((m, D_MODEL), BF16)),
        grid=(m // tm,),
        in_specs=[row, row,
                  pl.BlockSpec((None, 1, D_MODEL), lambda i: (layer, 0, 0)),
                  pl.BlockSpec((None, 1, D_MODEL), lambda i: (next_layer, 0, 0))],
        out_specs=(row, row),
        compiler_params=_params(1),
        name="post_norm",
    )(y, x, w_post, w_next)


def _bias_kernel(t_ref, bp_ref, bs_ref):
    t = t_ref[...]
    b1 = t.astype(BF16)
    r1 = t - b1.astype(F32)
    b2 = r1.astype(BF16)
    b3 = (r1 - b2.astype(F32)).astype(BF16)
    width = ATT_KW + ATT_G
    r = lax.broadcasted_iota(jnp.int32, (384, width), 0)
    mm = lax.broadcasted_iota(jnp.int32, (384, width), 1)
    idx = jnp.clip(ATT_BAND + ATT_G - mm, -REL_CLIP, REL_CLIP) + REL_CLIP
    onehot = jnp.where(idx == r, 1.0, 0.0).astype(BF16)
    g = (jnp.dot(b1, onehot, preferred_element_type=F32)
         + jnp.dot(b2, onehot, preferred_element_type=F32)
         + jnp.dot(b3, onehot, preferred_element_type=F32))
    i = lax.broadcasted_iota(jnp.int32, (ATT_G, ATT_KW), 0)
    j = lax.broadcasted_iota(jnp.int32, (ATT_G, ATT_KW), 1)
    rel = jnp.right_shift(i, 6) + 8 - jnp.right_shift(j, 6)
    valid = (rel >= 0) & (rel <= 8)
    js = lax.broadcasted_iota(jnp.int32, (TS, SAMPLE_KW), 1)
    for hd in range(ATT_HEADS):
        gb = jnp.broadcast_to(g[hd:hd + 1, :], (ATT_G, width))
        y = pltpu.roll(gb, ATT_KW, 1, stride=1, stride_axis=0)
        bp_ref[hd] = jnp.where(valid, y[:, :ATT_KW], NEG_INF)
        bs_ref[hd] = jnp.where(js < ATT_BAND + TS, y[:TS, :SAMPLE_KW], NEG_INF)


def _rel_bias_tiles(table):
    tt = jnp.transpose(table, (0, 2, 1))
    tt = jnp.pad(tt, ((0, 0), (0, 0), (0, 384 - tt.shape[2])))
    return pl.pallas_call(
        _bias_kernel,
        out_shape=(jax.ShapeDtypeStruct((DEPTH, ATT_HEADS, ATT_G, ATT_KW), F32),
                   jax.ShapeDtypeStruct((DEPTH, ATT_HEADS, TS, SAMPLE_KW), F32)),
        grid=(DEPTH,),
        in_specs=[pl.BlockSpec((None, ATT_HEADS, 384), lambda l: (l, 0, 0))],
        out_specs=(pl.BlockSpec((None, ATT_HEADS, ATT_G, ATT_KW), lambda l: (l, 0, 0, 0)),
                   pl.BlockSpec((None, ATT_HEADS, TS, SAMPLE_KW), lambda l: (l, 0, 0, 0))),
        compiler_params=_params(1),
        name="rel_bias_tiles",
    )(tt)


def _softmax_pv(s, v):
    m = jnp.max(s, axis=-1, keepdims=True)
    p = jnp.exp(s - m)
    denom = jnp.sum(p, axis=-1, keepdims=True)
    o = jnp.dot(p.astype(BF16), v, preferred_element_type=F32)
    return o / denom


def _attn_prompt_kernel(q_ref, k_ref, v_ref, b_ref, o_ref, kp_ref, vp_ref):
    zeros = jnp.zeros((ATT_BAND, ATT_DH), BF16)
    kp_ref[pl.ds(0, ATT_BAND), :] = zeros
    vp_ref[pl.ds(0, ATT_BAND), :] = zeros
    kp_ref[pl.ds(ATT_BAND, TP), :] = k_ref[...]
    vp_ref[pl.ds(ATT_BAND, TP), :] = v_ref[...]
    col = lax.broadcasted_iota(jnp.int32, (ATT_G, ATT_KW), 1)

    def body(g, carry):
        start = pl.multiple_of(g * ATT_G, ATT_G)
        q = q_ref[pl.ds(start, ATT_G), :]
        kb = kp_ref[pl.ds(start, ATT_KW), :]
        vb = vp_ref[pl.ds(start, ATT_KW), :]
        s = lax.dot_general(q, kb, (((1,), (1,)), ((), ())), preferred_element_type=F32)
        s = s * (ATT_DH ** -0.5) + b_ref[...]
        s = jnp.where(col >= ATT_BAND - start, s, NEG_INF)
        o_ref[pl.ds(start, ATT_G), :] = _softmax_pv(s, vb).astype(o_ref.dtype)
        return carry

    lax.fori_loop(0, TP // ATT_G, body, 0)


def _attn_prompt(h, bias_p, layer):
    def col_spec(off):
        return pl.BlockSpec((TP, ATT_DH), lambda b, hd: (b, off // ATT_DH + hd))

    return pl.pallas_call(
        _attn_prompt_kernel,
        out_shape=jax.ShapeDtypeStruct((M_ALL, ATT_WIDTH), BF16),
        grid=(BP, ATT_HEADS),
        in_specs=[col_spec(QA), col_spec(KA), col_spec(VA),
                  pl.BlockSpec((None, None, ATT_G, ATT_KW), lambda b, hd: (layer, hd, 0, 0))],
        out_specs=pl.BlockSpec((TP, ATT_DH), lambda b, hd: (b, hd)),
        scratch_shapes=[pltpu.VMEM((ATT_BAND + TP, ATT_DH), BF16),
                        pltpu.VMEM((ATT_BAND + TP, ATT_DH), BF16)],
        compiler_params=_params(2),
        name="attn_prompt",
    )(h, h, h, bias_p)


def _attn_sample_kernel(q_ref, k_ref, v_ref, ck_ref, cv_ref, b_ref, ya_hbm, o_ref, kk_ref, vv_ref):
    del ya_hbm
    pad = jnp.zeros((SAMPLE_KW - ATT_BAND - TS, ATT_DH), BF16)
    kk_ref[pl.ds(0, ATT_BAND), :] = ck_ref[...].astype(BF16)
    vv_ref[pl.ds(0, ATT_BAND), :] = cv_ref[...].astype(BF16)
    kk_ref[pl.ds(ATT_BAND, TS), :] = k_ref[...]
    vv_ref[pl.ds(ATT_BAND, TS), :] = v_ref[...]
    kk_ref[pl.ds(ATT_BAND + TS, SAMPLE_KW - ATT_BAND - TS), :] = pad
    vv_ref[pl.ds(ATT_BAND + TS, SAMPLE_KW - ATT_BAND - TS), :] = pad
    s = lax.dot_general(q_ref[...], kk_ref[...], (((1,), (1,)), ((), ())), preferred_element_type=F32)
    s = s * (ATT_DH ** -0.5) + b_ref[...]
    o_ref[...] = _softmax_pv(s, vv_ref[...]).astype(o_ref.dtype)


def _attn_sample(h, cache_k, cache_v, bias_s, layer, ya):
    row0 = MP // TS

    def col_spec(off):
        return pl.BlockSpec((TS, ATT_DH), lambda b, hd: (row0 + b, off // ATT_DH + hd))

    cache_spec = pl.BlockSpec((None, None, ATT_BAND, ATT_DH), lambda b, hd: (layer, b, 0, hd))
    return pl.pallas_call(
        _attn_sample_kernel,
        out_shape=jax.ShapeDtypeStruct(ya.shape, ya.dtype),
        grid=(BS, ATT_HEADS),
        in_specs=[col_spec(QA), col_spec(KA), col_spec(VA), cache_spec, cache_spec,
                  pl.BlockSpec((None, None, TS, SAMPLE_KW), lambda b, hd: (layer, hd, 0, 0)),
                  pl.BlockSpec(memory_space=pl.ANY)],
        out_specs=pl.BlockSpec((TS, ATT_DH), lambda b, hd: (row0 + b, hd)),
        scratch_shapes=[pltpu.VMEM((SAMPLE_KW, ATT_DH), BF16), pltpu.VMEM((SAMPLE_KW, ATT_DH), BF16)],
        input_output_aliases={6: 0},
        compiler_params=_params(2),
        name="attn_sample",
    )(h, h, h, cache_k, cache_v, bias_s, ya)


def _retention_consts(block, length):
    lg = jnp.log1p(-jnp.exp2(-5.0 - jnp.arange(RET_HEADS, dtype=F32)))
    i = jnp.arange(block, dtype=F32)
    diff = i[:, None] - i[None, :]
    dmat = jnp.where(diff >= 0, jnp.exp(lg[:, None, None] * jnp.maximum(diff, 0.0)), 0.0)
    q_decay = jnp.exp(lg[:, None] * (i[None, :] + 1.0))[:, :, None]
    k_decay = jnp.where(i[None, :] < length,
                        jnp.exp(lg[:, None] * jnp.maximum(length - 1.0 - i[None, :], 0.0)), 0.0)[:, :, None]
    s_decay = jnp.exp(lg * length)[:, None, None]
    return dmat, q_decay, k_decay, s_decay


def _rope_tables(pos):
    half = RET_DK // 2
    inv = 1.0 / (ROPE_BASE ** (jnp.arange(half, dtype=F32) / half))
    ang = pos.astype(F32)[:, None] * inv[None, :]
    return jnp.cos(ang), jnp.sin(ang)


def _rotary(x, cos, sin):
    half = RET_DK // 2
    x1, x2 = x[:, :half], x[:, half:]
    return jnp.concatenate([x1 * cos - x2 * sin, x1 * sin + x2 * cos], axis=-1)


def _retention_block(q, k, v, g, cos, sin, s, dmat, q_decay, k_decay, s_decay, gn_w):
    qb = _rotary(q, cos, sin).astype(BF16)
    kr = _rotary(k, cos, sin) * (RET_DK ** -0.5)
    scores = lax.dot_general(qb, kr.astype(BF16), (((1,), (1,)), ((), ())),
                             preferred_element_type=F32) * dmat
    o = (jnp.dot(scores.astype(BF16), v, preferred_element_type=F32)
         + jnp.dot(qb, s.astype(BF16), preferred_element_type=F32) * q_decay)
    kt = jnp.transpose(kr * k_decay).astype(BF16)
    s_new = s_decay * s + jnp.dot(kt, v, preferred_element_type=F32)
    mu = jnp.mean(o, axis=-1, keepdims=True)
    var = jnp.mean(jnp.square(o - mu), axis=-1, keepdims=True)
    y = (o - mu) * lax.rsqrt(var + 1e-5) * gn_w * _silu(g)
    return y, s_new


def _ret_prompt_kernel(q_ref, k_ref, v_ref, g_ref, cos_ref, sin_ref, dm_ref, qd_ref, kd_ref, sd_ref,
                       gn_ref, o_ref, so_ref, s_ref):
    s_ref[...] = jnp.zeros_like(s_ref)

    def body(c, carry):
        st = pl.multiple_of(c * RET_L, RET_L)
        rows = pl.ds(st, RET_L)
        y, s_new = _retention_block(
            q_ref[rows, :].astype(F32), k_ref[rows, :].astype(F32), v_ref[rows, :],
            g_ref[rows, :].astype(F32), cos_ref[rows, :], sin_ref[rows, :], s_ref[...],
            dm_ref[...], qd_ref[...], kd_ref[...], sd_ref[...], gn_ref[...])
        o_ref[rows, :] = y.astype(o_ref.dtype)
        s_ref[...] = s_new
        return carry

    lax.fori_loop(0, TP // RET_L, body, 0)
    so_ref[...] = s_ref[...]


def _head_const_specs(block):
    return [pl.BlockSpec((None, block, block), lambda b, hd: (hd, 0, 0)),
            pl.BlockSpec((None, block, 1), lambda b, hd: (hd, 0, 0)),
            pl.BlockSpec((None, block, 1), lambda b, hd: (hd, 0, 0)),
            pl.BlockSpec((None, 1, 1), lambda b, hd: (hd, 0, 0))]


def _ret_prompt(h, cos, sin, consts, gn_w, layer):
    def col_spec(off, width):
        return pl.BlockSpec((TP, width), lambda b, hd: (b, off // width + hd))

    table = pl.BlockSpec((TP, RET_DK // 2), lambda b, hd: (0, 0))
    return pl.pallas_call(
        _ret_prompt_kernel,
        out_shape=(jax.ShapeDtypeStruct((M_ALL, RET_V_WIDTH), BF16),
                   jax.ShapeDtypeStruct((BP, RET_HEADS, RET_DK, RET_DV), F32)),
        grid=(BP, RET_HEADS),
        in_specs=[col_spec(QR, RET_DK), col_spec(KR, RET_DK), col_spec(VR, RET_DV), col_spec(GR, RET_DV),
                  table, table, *_head_const_specs(RET_L),
                  pl.BlockSpec((None, 1, RET_DV), lambda b, hd: (layer, 0, hd))],
        out_specs=(pl.BlockSpec((TP, RET_DV), lambda b, hd: (b, hd)),
                   pl.BlockSpec((None, None, RET_DK, RET_DV), lambda b, hd: (b, hd, 0, 0))),
        scratch_shapes=[pltpu.VMEM((RET_DK, RET_DV), F32)],
        compiler_params=_params(2),
        name="retention_prompt",
    )(h, h, h, h, cos, sin, *consts, gn_w)


def _pad_rows(x, rows):
    return jnp.concatenate([x, jnp.zeros((rows - x.shape[0], x.shape[1]), x.dtype)], axis=0)


def _ret_sample_kernel(q_ref, k_ref, v_ref, g_ref, cos_ref, sin_ref, dm_ref, qd_ref, kd_ref, sd_ref,
                       gn_ref, s0_ref, yr_hbm, o_ref, so_ref):
    del yr_hbm
    y, s_new = _retention_block(
        _pad_rows(q_ref[...].astype(F32), SAMPLE_L), _pad_rows(k_ref[...].astype(F32), SAMPLE_L),
        _pad_rows(v_ref[...].astype(F32), SAMPLE_L).astype(BF16),
        _pad_rows(g_ref[...].astype(F32), SAMPLE_L), cos_ref[...], sin_ref[...], s0_ref[...],
        dm_ref[...], qd_ref[...], kd_ref[...], sd_ref[...], gn_ref[...])
    o_ref[...] = y[:TS].astype(o_ref.dtype)
    so_ref[...] = s_new


def _ret_sample(h, cos, sin, consts, gn_w, state, layer, yr):
    row0 = MP // TS

    def col_spec(off, width):
        return pl.BlockSpec((TS, width), lambda b, hd: (row0 + b, off // width + hd))

    table = pl.BlockSpec((SAMPLE_L, RET_DK // 2), lambda b, hd: (0, 0))
    return pl.pallas_call(
        _ret_sample_kernel,
        out_shape=(jax.ShapeDtypeStruct(yr.shape, yr.dtype),
                   jax.ShapeDtypeStruct((BS, RET_HEADS, RET_DK, RET_DV), F32)),
        grid=(BS, RET_HEADS),
        in_specs=[col_spec(QR, RET_DK), col_spec(KR, RET_DK), col_spec(VR, RET_DV), col_spec(GR, RET_DV),
                  table, table, *_head_const_specs(SAMPLE_L),
                  pl.BlockSpec((None, 1, RET_DV), lambda b, hd: (layer, 0, hd)),
                  pl.BlockSpec((None, None, None, RET_DK, RET_DV), lambda b, hd: (layer, b, hd, 0, 0)),
                  pl.BlockSpec(memory_space=pl.ANY)],
        out_specs=(pl.BlockSpec((TS, RET_DV), lambda b, hd: (row0 + b, hd)),
                   pl.BlockSpec((None, None, RET_DK, RET_DV), lambda b, hd: (b, hd, 0, 0))),
        input_output_aliases={12: 0},
        compiler_params=_params(2),
        name="retention_sample",
    )(h, h, h, h, cos, sin, *consts, gn_w, state, yr)


def _conv_kernel(*refs, tt, aliased):
    if aliased:
        ca_ref, cb_ref, hist_ref, w_ref, b_ref, lnw_ref, lnb_ref, _, o_ref, tail_ref, u_ref, y_ref = refs
    else:
        ca_ref, cb_ref, hist_ref, w_ref, b_ref, lnw_ref, lnb_ref, o_ref, tail_ref, u_ref, y_ref = refs
    t = pl.program_id(1)

    @pl.when(t == 0)
    def _():
        u_ref[pl.ds(0, CONV_HIST), :] = hist_ref[...]

    u_ref[pl.ds(CONV_HIST, tt), :] = ca_ref[...].astype(F32) * _sigmoid(cb_ref[...].astype(F32))

    rows = min(tt, 32)
    lanes = 512
    for r0 in range(0, tt, rows):
        for c0 in range(0, CONV_C, lanes):
            acc = jnp.zeros((rows, lanes), F32)
            for k in range(CONV_W):
                acc = acc + (u_ref[pl.ds(r0 + 2 + k, rows), pl.ds(c0, lanes)]
                             * w_ref[pl.ds(k, 1), pl.ds(c0, lanes)])
            y_ref[pl.ds(r0, rows), pl.ds(c0, lanes)] = acc

    y = y_ref[...] + b_ref[...]
    mu = jnp.mean(y, axis=-1, keepdims=True)
    var = jnp.mean(jnp.square(y - mu), axis=-1, keepdims=True)
    y = (y - mu) * lax.rsqrt(var + 1e-5) * lnw_ref[...] + lnb_ref[...]
    o_ref[...] = _silu(y).astype(o_ref.dtype)

    tail = u_ref[pl.ds(tt, CONV_HIST), :]
    tail_ref[...] = tail
    u_ref[pl.ds(0, CONV_HIST), :] = tail


def _conv(h, hist, dw_w, dw_b, ln_w, ln_b, layer, *, batch, steps, tt, row0, yc=None):
    def col_spec(off):
        return pl.BlockSpec((tt, CONV_C), lambda b, t: (row0 + b * steps + t, off // CONV_C))

    def vec_spec():
        return pl.BlockSpec((None, 1, CONV_C), lambda b, t: (layer, 0, 0))

    in_specs = [col_spec(CA), col_spec(CB),
                pl.BlockSpec((None, CONV_HIST, CONV_C), lambda b, t: (b, 0, 0)),
                pl.BlockSpec((None, 32, CONV_C), lambda b, t: (layer, 0, 0)),
                vec_spec(), vec_spec(), vec_spec()]
    args = [h, h, hist, dw_w, dw_b, ln_w, ln_b]
    aliases = {}
    if yc is not None:
        in_specs.append(pl.BlockSpec(memory_space=pl.ANY))
        args.append(yc)
        aliases = {7: 0}
    return pl.pallas_call(
        functools.partial(_conv_kernel, tt=tt, aliased=yc is not None),
        out_shape=(jax.ShapeDtypeStruct((M_ALL, CONV_C), BF16),
                   jax.ShapeDtypeStruct((batch, CONV_HIST, CONV_C), F32)),
        grid=(batch, steps),
        in_specs=in_specs,
        out_specs=(pl.BlockSpec((tt, CONV_C), lambda b, t: (row0 + b * steps + t, 0)),
                   pl.BlockSpec((None, CONV_HIST, CONV_C), lambda b, t: (b, 0, 0))),
        scratch_shapes=[pltpu.VMEM((CONV_HIST + tt, CONV_C), F32), pltpu.VMEM((tt, CONV_C), F32)],
        input_output_aliases=aliases,
        compiler_params=_params(2),
        name="conv_prompt" if yc is None else "conv_sample",
    )(*args)


def kernel(x_prompt, x_sample, cache_attn_k, cache_attn_v, state_ret, cache_conv, norm_mix_pre, norm_mix_post, norm_ffn_pre, norm_ffn_post, w_in, attn_rel_bias, ret_gn_w, conv_dw_w, conv_dw_b, conv_ln_w, conv_ln_b, w_branch_attn, w_branch_ret, w_branch_conv, w_out, w_ffn_gate, w_ffn_up, w_ffn_down):
    x = jnp.concatenate([x_prompt.reshape(MP, D_MODEL), x_sample.reshape(BS * TS, D_MODEL)], axis=0)

    def row_vec(w):
        return w.reshape(DEPTH, 1, w.shape[-1])

    n_mp, n_mq, n_fp, n_fq = map(row_vec, (norm_mix_pre, norm_mix_post, norm_ffn_pre, norm_ffn_post))
    gn_w, dw_b, ln_w, ln_b = map(row_vec, (ret_gn_w, conv_dw_b, conv_ln_w, conv_ln_b))
    dw_w = jnp.pad(conv_dw_w, ((0, 0), (0, 32 - CONV_W), (0, 0)))
    cache_k = cache_attn_k.reshape(DEPTH, BS, ATT_BAND, ATT_WIDTH)
    cache_v = cache_attn_v.reshape(DEPTH, BS, ATT_BAND, ATT_WIDTH)
    hist_p = jnp.zeros((BP, CONV_HIST, CONV_C), F32)
    hist_s = jnp.pad(cache_conv, ((0, 0), (0, 0), (CONV_HIST - (CONV_W - 1), 0), (0, 0)))

    bias_p, bias_s = _rel_bias_tiles(attn_rel_bias)
    cos_p, sin_p = _rope_tables(jnp.arange(TP))
    cos_s, sin_s = _rope_tables(PAST_LEN + jnp.arange(TS))
    cos_s = jnp.pad(cos_s, ((0, SAMPLE_L - TS), (0, 0)))
    sin_s = jnp.pad(sin_s, ((0, SAMPLE_L - TS), (0, 0)))
    ret_p = _retention_consts(RET_L, RET_L)
    ret_s = _retention_consts(SAMPLE_L, TS)

    kp_l, vp_l, sp_l, cp_l, ks_l, vs_l, ss_l, cs_l = ([] for _ in range(8))
    xn = _pre_norm(x, n_mp, 0)
    for l in range(DEPTH):
        h = _proj(xn, w_in, l, k_size=D_MODEL, k_block=0, tn=512, out_dtype=BF16, name="in_proj")

        ya = _attn_prompt(h, bias_p, l)
        ya = _attn_sample(h, cache_k, cache_v, bias_s, l, ya)
        yr, s_p = _ret_prompt(h, cos_p, sin_p, ret_p, gn_w, l)
        yr, s_s = _ret_sample(h, cos_s, sin_s, ret_s, gn_w, state_ret, l, yr)
        yc, tail_p = _conv(h, hist_p, dw_w, dw_b, ln_w, ln_b, l,
                           batch=BP, steps=TP // CONV_TT, tt=CONV_TT, row0=0)
        yc, tail_s = _conv(h, hist_s[l], dw_w, dw_b, ln_w, ln_b, l,
                           batch=BS, steps=1, tt=TS, row0=MP // TS, yc=yc)

        merged = _merge(ya, yr, yc, h, w_branch_attn, w_branch_ret, w_branch_conv, l)
        y = _proj(merged, w_out, l, k_size=D_MODEL, k_block=0, tn=512, out_dtype=BF16, name="out_proj")
        x, xn = _post_norm(y, x, n_mq, l, n_fp, l)

        hidden = _swiglu_up(xn, w_ffn_gate, w_ffn_up, l)
        half = FFN_HIDDEN // 2
        y = _proj(hidden, w_ffn_down, l, k_size=half, k_block=0, tn=256, out_dtype=F32, name="ffn_down_lo")
        y = _proj(hidden, w_ffn_down, l, k_size=half, k_block=1, tn=256, out_dtype=F32, addend=y,
                  name="ffn_down_hi")
        x, xn = _post_norm(y, x, n_fq, l, n_mp, min(l + 1, DEPTH - 1))

        hp = h[:MP].reshape(BP, TP, N_IN)
        hs = h[MP:].reshape(BS, TS, N_IN)
        keep = TP - ATT_BAND
        kp_l.append(hp[:, keep:, KA:KA + ATT_WIDTH].astype(F32).reshape(BP, ATT_BAND, ATT_HEADS, ATT_DH))
        vp_l.append(hp[:, keep:, VA:VA + ATT_WIDTH].astype(F32).reshape(BP, ATT_BAND, ATT_HEADS, ATT_DH))
        ks_l.append(hs[:, :, KA:KA + ATT_WIDTH].astype(F32).reshape(BS, TS, ATT_HEADS, ATT_DH))
        vs_l.append(hs[:, :, VA:VA + ATT_WIDTH].astype(F32).reshape(BS, TS, ATT_HEADS, ATT_DH))
        sp_l.append(s_p)
        ss_l.append(s_s)
        cp_l.append(tail_p[:, CONV_HIST - (CONV_W - 1):])
        cs_l.append(tail_s[:, CONV_HIST - (CONV_W - 1):])

    return (x[:MP].reshape(BP, TP, D_MODEL), x[MP:].reshape(BS, TS, D_MODEL),
            jnp.stack(kp_l), jnp.stack(vp_l), jnp.stack(sp_l), jnp.stack(cp_l),
            jnp.stack(ks_l), jnp.stack(vs_l), jnp.stack(ss_l), jnp.stack(cs_l))
```

```python
import functools

import jax
import jax.numpy as jnp
from jax import lax
from jax.experimental import pallas as pl
from jax.experimental.pallas import tpu as pltpu

F32 = jnp.float32
BF16 = jnp.bfloat16

D_MODEL = 4096
DEPTH = 4
BP, TP = 4, 2048
BS, TS = 8, 16
PAST_LEN = 2048
MP = BP * TP
M_ALL = MP + BS * TS

CHUNK = 64
ATT_HEADS, ATT_DH = 16, 128
ATT_WIDTH = ATT_HEADS * ATT_DH
ATT_BAND = 8 * CHUNK
REL_CLIP = 128
RET_HEADS, RET_DK, RET_DV = 8, 256, 512
RET_V_WIDTH = RET_HEADS * RET_DV
ROPE_BASE = 10000.0
CONV_C, CONV_W = 2048, 31
FFN_HIDDEN = 11008
NEG_INF = -1e30

QA, KA, VA = 0, 2048, 4096
QR, KR, VR, GR = 6144, 8192, 10240, 14336
CA, CB = 18432, 20480
GA, GB, GC = 22528, 26624, 30720
N_IN = 34816

VMEM_LIMIT_BYTES = 56 * 1024 * 1024

TM = 832
W_CHUNKS = 8
RET_L = 256
ATT_G = 256
ATT_KW = ATT_G + ATT_BAND
SAMPLE_KW = 640
SAMPLE_L = 128
CONV_TT = 64
CONV_HIST = 32
CONV_LANES = 256


def _params(n_axes):
    return pltpu.CompilerParams(dimension_semantics=("arbitrary",) * n_axes,
                                vmem_limit_bytes=VMEM_LIMIT_BYTES)


def _sigmoid(x):
    return 1.0 / (1.0 + jnp.exp(-x))


def _silu(x):
    return x * _sigmoid(x)


def _cast_weight(w_ref, wb_ref):
    def body(c, carry):
        r = pl.multiple_of(c * 128, 128)
        wb_ref[pl.ds(r, 128), :] = w_ref[pl.ds(r, 128), :].astype(BF16)
        return carry
    lax.fori_loop(0, w_ref.shape[0] // 128, body, 0)


def _tok_block(jj, s, steps):
    return jnp.minimum(s, jj * steps)


def _col_block(jj):
    return jnp.maximum(jj - 1, 0)


def _weight_spec(layer, k, tn, n_tiles):
    def index(jj, s):
        chunk = jnp.where(jj >= n_tiles, W_CHUNKS - 1, jnp.minimum(s, W_CHUNKS - 1))
        return (layer, chunk, jnp.minimum(jj, n_tiles - 1))
    return pl.BlockSpec((None, k // W_CHUNKS, tn), index)


def _stage_weights(w_refs, wb_refs, n_tiles):
    jj, s = pl.program_id(0), pl.program_id(1)

    @pl.when((s < W_CHUNKS) & (jj < n_tiles))
    def _():
        for w_ref, wb_ref in zip(w_refs, wb_refs):
            rows = w_ref.shape[0]
            r = pl.multiple_of(s * rows, 16)
            wb_ref[jj % 2, pl.ds(r, rows), :] = w_ref[...].astype(BF16)


def _proj_kernel(x_ref, w_ref, o_ref, wb_ref, *, n_tiles):
    _stage_weights([w_ref], [wb_ref], n_tiles)
    jj = pl.program_id(0)

    @pl.when(jj > 0)
    def _():
        w = wb_ref[(jj + 1) % 2]
        o_ref[...] = jnp.dot(x_ref[...], w, preferred_element_type=F32).astype(o_ref.dtype)


def _proj(x, w, layer, *, tn, tm, name):
    m, k = x.shape
    n = w.shape[2]
    n_tiles, steps = n // tn, m // tm
    assert steps >= W_CHUNKS and k % (16 * W_CHUNKS) == 0 and n % tn == 0 and m % tm == 0
    return pl.pallas_call(
        functools.partial(_proj_kernel, n_tiles=n_tiles),
        out_shape=jax.ShapeDtypeStruct((m, n), BF16),
        grid=(n_tiles + 1, steps),
        in_specs=[pl.BlockSpec((tm, k), lambda jj, s: (_tok_block(jj, s, steps), 0)),
                  _weight_spec(layer, k, tn, n_tiles)],
        out_specs=pl.BlockSpec((tm, tn), lambda jj, s: (_tok_block(jj, s, steps), _col_block(jj))),
        scratch_shapes=[pltpu.VMEM((2, k, tn), BF16)],
        compiler_params=_params(2),
        name=name,
    )(x, w)


def _swiglu_kernel(x_ref, wg_ref, wu_ref, o_ref, wgb_ref, wub_ref):
    @pl.when(pl.program_id(1) == 0)
    def _():
        _cast_weight(wg_ref, wgb_ref)
        _cast_weight(wu_ref, wub_ref)

    x = x_ref[...]
    a = jnp.dot(x, wgb_ref[...], preferred_element_type=F32)
    b = jnp.dot(x, wub_ref[...], preferred_element_type=F32)
    o_ref[...] = (_silu(a) * b).astype(o_ref.dtype)


def _swiglu_up(xn, w_gate, w_up, layer):
    tn = 256
    m = xn.shape[0]
    wspec = pl.BlockSpec((None, D_MODEL, tn), lambda j, i: (layer, 0, j))
    return pl.pallas_call(
        _swiglu_kernel,
        out_shape=jax.ShapeDtypeStruct((m, FFN_HIDDEN), BF16),
        grid=(FFN_HIDDEN // tn, m // TM),
        in_specs=[pl.BlockSpec((TM, D_MODEL), lambda j, i: (i, 0)), wspec, wspec],
        out_specs=pl.BlockSpec((TM, tn), lambda j, i: (i, j)),
        scratch_shapes=[pltpu.VMEM((D_MODEL, tn), BF16), pltpu.VMEM((D_MODEL, tn), BF16)],
        compiler_params=_params(2),
        name="ffn_gate_up",
    )(xn, w_gate, w_up)


def _merge_kernel(ya_ref, yr_ref, yc_ref, ga_ref, gb_ref, gc_ref, wa_ref, wr_ref, wc_ref,
                  o_ref, wab_ref, wrb_ref, wcb_ref, *, n_tiles):
    _stage_weights([wa_ref, wr_ref, wc_ref], [wab_ref, wrb_ref, wcb_ref], n_tiles)
    jj = pl.program_id(0)

    @pl.when(jj > 0)
    def _():
        slot = (jj + 1) % 2
        a = jnp.dot(ya_ref[...], wab_ref[slot], preferred_element_type=F32)
        r = jnp.dot(yr_ref[...], wrb_ref[slot], preferred_element_type=F32)
        c = jnp.dot(yc_ref[...], wcb_ref[slot], preferred_element_type=F32)
        out = (_sigmoid(ga_ref[...].astype(F32)) * a + _sigmoid(gb_ref[...].astype(F32)) * r
               + _sigmoid(gc_ref[...].astype(F32)) * c)
        o_ref[...] = out.astype(o_ref.dtype)


def _merge(ya, yr, yc, h, w_ba, w_br, w_bc, layer):
    tn = 512
    tm = 416
    m = ya.shape[0]
    n_tiles, steps = D_MODEL // tn, m // tm
    assert steps >= W_CHUNKS and m % tm == 0

    def gate_spec(off):
        return pl.BlockSpec((tm, tn),
                            lambda jj, s: (_tok_block(jj, s, steps), off // tn + _col_block(jj)))

    def x_spec(k):
        return pl.BlockSpec((tm, k), lambda jj, s: (_tok_block(jj, s, steps), 0))

    return pl.pallas_call(
        functools.partial(_merge_kernel, n_tiles=n_tiles),
        out_shape=jax.ShapeDtypeStruct((m, D_MODEL), BF16),
        grid=(n_tiles + 1, steps),
        in_specs=[x_spec(ATT_WIDTH), x_spec(RET_V_WIDTH), x_spec(CONV_C),
                  gate_spec(GA), gate_spec(GB), gate_spec(GC),
                  _weight_spec(layer, ATT_WIDTH, tn, n_tiles),
                  _weight_spec(layer, RET_V_WIDTH, tn, n_tiles),
                  _weight_spec(layer, CONV_C, tn, n_tiles)],
        out_specs=pl.BlockSpec((tm, tn), lambda jj, s: (_tok_block(jj, s, steps), _col_block(jj))),
        scratch_shapes=[pltpu.VMEM((2, ATT_WIDTH, tn), BF16), pltpu.VMEM((2, RET_V_WIDTH, tn), BF16),
                        pltpu.VMEM((2, CONV_C, tn), BF16)],
        compiler_params=_params(2),
        name="branch_merge",
    )(ya, yr, yc, h, h, h, w_ba, w_br, w_bc)


def _rms(x, w, eps=1e-6):
    return x * lax.rsqrt(jnp.mean(x * x, axis=-1, keepdims=True) + eps) * w


def _pre_norm_kernel(x_ref, w_ref, o_ref):
    o_ref[...] = _rms(x_ref[...], w_ref[...]).astype(o_ref.dtype)


def _pre_norm(x, w, layer):
    tm = 208
    m = x.shape[0]
    return pl.pallas_call(
        _pre_norm_kernel,
        out_shape=jax.ShapeDtypeStruct((m, D_MODEL), BF16),
        grid=(m // tm,),
        in_specs=[pl.BlockSpec((tm, D_MODEL), lambda i: (i, 0)),
                  pl.BlockSpec((None, 1, D_MODEL), lambda i: (layer, 0, 0))],
        out_specs=pl.BlockSpec((tm, D_MODEL), lambda i: (i, 0)),
        compiler_params=_params(1),
        name="pre_norm",
    )(x, w)


def _post_norm_kernel(y_ref, x_ref, wp_ref, wn_ref, xo_ref, xn_ref):
    x_new = x_ref[...] + _rms(y_ref[...].astype(F32), wp_ref[...])
    xo_ref[...] = x_new
    xn_ref[...] = _rms(x_new, wn_ref[...]).astype(xn_ref.dtype)


def _post_norm(y, x, w_post, layer, w_next, next_layer):
    tm = 208
    m = x.shape[0]
    row = pl.BlockSpec((tm, D_MODEL), lambda i: (i, 0))
    return pl.pallas_call(
        _post_norm_kernel,
        out_shape=(jax.ShapeDtypeStruct((m, D_MODEL), F32),
                   jax.ShapeDtypeStruct((m, D_MODEL), BF16)),
        grid=(m // tm,),
        in_specs=[row, row,
                  pl.BlockSpec((None, 1, D_MODEL), lambda i: (layer, 0, 0)),
                  pl.BlockSpec((None, 1, D_MODEL), lambda i: (next_layer, 0, 0))],
        out_specs=(row, row),
        compiler_params=_params(1),
        name="post_norm",
    )(y, x, w_post, w_next)


def _bias_kernel(t_ref, bp_ref, bs_ref):
    t = t_ref[...]
    b1 = t.astype(BF16)
    r1 = t - b1.astype(F32)
    b2 = r1.astype(BF16)
    b3 = (r1 - b2.astype(F32)).astype(BF16)
    width = ATT_KW + ATT_G
    r = lax.broadcasted_iota(jnp.int32, (384, width), 0)
    mm = lax.broadcasted_iota(jnp.int32, (384, width), 1)
    idx = jnp.clip(ATT_BAND + ATT_G - mm, -REL_CLIP, REL_CLIP) + REL_CLIP
    onehot = jnp.where(idx == r, 1.0, 0.0).astype(BF16)
    g = (jnp.dot(b1, onehot, preferred_element_type=F32)
         + jnp.dot(b2, onehot, preferred_element_type=F32)
         + jnp.dot(b3, onehot, preferred_element_type=F32))
    i = lax.broadcasted_iota(jnp.int32, (ATT_G, ATT_KW), 0)
    j = lax.broadcasted_iota(jnp.int32, (ATT_G, ATT_KW), 1)
    rel = jnp.right_shift(i, 6) + 8 - jnp.right_shift(j, 6)
    valid = (rel >= 0) & (rel <= 8)
    js = lax.broadcasted_iota(jnp.int32, (TS, SAMPLE_KW), 1)
    for hd in range(ATT_HEADS):
        gb = jnp.broadcast_to(g[hd:hd + 1, :], (ATT_G, width))
        y = pltpu.roll(gb, ATT_KW, 1, stride=1, stride_axis=0)
        bp_ref[hd] = jnp.where(valid, y[:, :ATT_KW], NEG_INF)
        bs_ref[hd] = jnp.where(js < ATT_BAND + TS, y[:TS, :SAMPLE_KW], NEG_INF)


def _rel_bias_tiles(table):
    tt = jnp.transpose(table, (0, 2, 1))
    tt = jnp.pad(tt, ((0, 0), (0, 0), (0, 384 - tt.shape[2])))
    return pl.pallas_call(
        _bias_kernel,
        out_shape=(jax.ShapeDtypeStruct((DEPTH, ATT_HEADS, ATT_G, ATT_KW), F32),
                   jax.ShapeDtypeStruct((DEPTH, ATT_HEADS, TS, SAMPLE_KW), F32)),
        grid=(DEPTH,),
        in_specs=[pl.BlockSpec((None, ATT_HEADS, 384), lambda l: (l, 0, 0))],
        out_specs=(pl.BlockSpec((None, ATT_HEADS, ATT_G, ATT_KW), lambda l: (l, 0, 0, 0)),
                   pl.BlockSpec((None, ATT_HEADS, TS, SAMPLE_KW), lambda l: (l, 0, 0, 0))),
        compiler_params=_params(1),
        name="rel_bias_tiles",
    )(tt)


def _softmax_pv(s, v):
    m = jnp.max(s, axis=-1, keepdims=True)
    p = jnp.exp(s - m)
    denom = jnp.sum(p, axis=-1, keepdims=True)
    o = jnp.dot(p.astype(BF16), v, preferred_element_type=F32)
    return o / denom


def _attn_prompt_kernel(q_ref, k_ref, v_ref, b_ref, o_ref, kp_ref, vp_ref):
    zeros = jnp.zeros((ATT_BAND, ATT_DH), BF16)
    kp_ref[pl.ds(0, ATT_BAND), :] = zeros
    vp_ref[pl.ds(0, ATT_BAND), :] = zeros
    kp_ref[pl.ds(ATT_BAND, TP), :] = k_ref[...]
    vp_ref[pl.ds(ATT_BAND, TP), :] = v_ref[...]
    col = lax.broadcasted_iota(jnp.int32, (ATT_G, ATT_KW), 1)

    def body(g, carry):
        start = pl.multiple_of(g * ATT_G, ATT_G)
        q = q_ref[pl.ds(start, ATT_G), :]
        kb = kp_ref[pl.ds(start, ATT_KW), :]
        vb = vp_ref[pl.ds(start, ATT_KW), :]
        s = lax.dot_general(q, kb, (((1,), (1,)), ((), ())), preferred_element_type=F32)
        s = s * (ATT_DH ** -0.5) + b_ref[...]
        s = jnp.where(col >= ATT_BAND - start, s, NEG_INF)
        o_ref[pl.ds(start, ATT_G), :] = _softmax_pv(s, vb).astype(o_ref.dtype)
        return carry

    lax.fori_loop(0, TP // ATT_G, body, 0)


def _attn_prompt(h, bias_p, layer):
    def col_spec(off):
        return pl.BlockSpec((TP, ATT_DH), lambda b, hd: (b, off // ATT_DH + hd))

    return pl.pallas_call(
        _attn_prompt_kernel,
        out_shape=jax.ShapeDtypeStruct((M_ALL, ATT_WIDTH), BF16),
        grid=(BP, ATT_HEADS),
        in_specs=[col_spec(QA), col_spec(KA), col_spec(VA),
                  pl.BlockSpec((None, None, ATT_G, ATT_KW), lambda b, hd: (layer, hd, 0, 0))],
        out_specs=pl.BlockSpec((TP, ATT_DH), lambda b, hd: (b, hd)),
        scratch_shapes=[pltpu.VMEM((ATT_BAND + TP, ATT_DH), BF16),
                        pltpu.VMEM((ATT_BAND + TP, ATT_DH), BF16)],
        compiler_params=_params(2),
        name="attn_prompt",
    )(h, h, h, bias_p)


def _attn_sample_kernel(q_ref, k_ref, v_ref, ck_ref, cv_ref, b_ref, ya_hbm, o_ref, kk_ref, vv_ref):
    del ya_hbm
    n_pad = SAMPLE_KW - ATT_BAND - TS
    pad = jnp.zeros((n_pad, ATT_DH), BF16)
    kk_ref[pl.ds(ATT_BAND + TS, n_pad), :] = pad
    vv_ref[pl.ds(ATT_BAND + TS, n_pad), :] = pad
    for hd in range(ATT_HEADS):
        cols = pl.ds(hd * ATT_DH, ATT_DH)
        kk_ref[pl.ds(0, ATT_BAND), :] = ck_ref[:, hd, :].astype(BF16)
        vv_ref[pl.ds(0, ATT_BAND), :] = cv_ref[:, hd, :].astype(BF16)
        kk_ref[pl.ds(ATT_BAND, TS), :] = k_ref[:, cols]
        vv_ref[pl.ds(ATT_BAND, TS), :] = v_ref[:, cols]
        s = lax.dot_general(q_ref[:, cols], kk_ref[...], (((1,), (1,)), ((), ())),
                            preferred_element_type=F32)
        s = s * (ATT_DH ** -0.5) + b_ref[hd]
        o_ref[:, cols] = _softmax_pv(s, vv_ref[...]).astype(o_ref.dtype)


def _attn_sample(h, cache_k, cache_v, bias_s, layer, ya):
    row0 = MP // TS

    def col_spec(off):
        return pl.BlockSpec((TS, ATT_WIDTH), lambda b: (row0 + b, off // ATT_WIDTH))

    cache_spec = pl.BlockSpec((None, None, ATT_BAND, ATT_HEADS, ATT_DH), lambda b: (layer, b, 0, 0, 0))
    return pl.pallas_call(
        _attn_sample_kernel,
        out_shape=jax.ShapeDtypeStruct(ya.shape, ya.dtype),
        grid=(BS,),
        in_specs=[col_spec(QA), col_spec(KA), col_spec(VA), cache_spec, cache_spec,
                  pl.BlockSpec((None, ATT_HEADS, TS, SAMPLE_KW), lambda b: (layer, 0, 0, 0)),
                  pl.BlockSpec(memory_space=pl.ANY)],
        out_specs=pl.BlockSpec((TS, ATT_WIDTH), lambda b: (row0 + b, 0)),
        scratch_shapes=[pltpu.VMEM((SAMPLE_KW, ATT_DH), BF16), pltpu.VMEM((SAMPLE_KW, ATT_DH), BF16)],
        input_output_aliases={6: 0},
        compiler_params=_params(1),
        name="attn_sample",
    )(h, h, h, cache_k, cache_v, bias_s, ya)


def _retention_consts(block, length):
    lg = jnp.log1p(-jnp.exp2(-5.0 - jnp.arange(RET_HEADS, dtype=F32)))
    i = jnp.arange(block, dtype=F32)
    diff = i[:, None] - i[None, :]
    dmat = jnp.where(diff >= 0, jnp.exp(lg[:, None, None] * jnp.maximum(diff, 0.0)), 0.0)
    q_decay = jnp.exp(lg[:, None] * (i[None, :] + 1.0))[:, :, None]
    k_decay = jnp.where(i[None, :] < length,
                        jnp.exp(lg[:, None] * jnp.maximum(length - 1.0 - i[None, :], 0.0)), 0.0)[:, :, None]
    s_decay = jnp.exp(lg * length)[:, None, None]
    return dmat, q_decay, k_decay, s_decay


def _rope_tables(pos):
    half = RET_DK // 2
    inv = 1.0 / (ROPE_BASE ** (jnp.arange(half, dtype=F32) / half))
    ang = pos.astype(F32)[:, None] * inv[None, :]
    return jnp.cos(ang), jnp.sin(ang)


def _rotary(x, cos, sin):
    half = RET_DK // 2
    x1, x2 = x[:, :half], x[:, half:]
    return jnp.concatenate([x1 * cos - x2 * sin, x1 * sin + x2 * cos], axis=-1)


def _retention_block(q, k, v, g, cos, sin, s, dmat, q_decay, k_decay, s_decay, gn_w):
    qb = _rotary(q, cos, sin).astype(BF16)
    kr = _rotary(k, cos, sin) * (RET_DK ** -0.5)
    scores = lax.dot_general(qb, kr.astype(BF16), (((1,), (1,)), ((), ())),
                             preferred_element_type=F32) * dmat
    o = (jnp.dot(scores.astype(BF16), v, preferred_element_type=F32)
         + jnp.dot(qb, s.astype(BF16), preferred_element_type=F32) * q_decay)
    kt = jnp.transpose(kr * k_decay).astype(BF16)
    s_new = s_decay * s + jnp.dot(kt, v, preferred_element_type=F32)
    mu = jnp.mean(o, axis=-1, keepdims=True)
    var = jnp.mean(jnp.square(o - mu), axis=-1, keepdims=True)
    y = (o - mu) * lax.rsqrt(var + 1e-5) * gn_w * _silu(g)
    return y, s_new


def _ret_prompt_kernel(q_ref, k_ref, v_ref, g_ref, cos_ref, sin_ref, dm_ref, qd_ref, kd_ref, sd_ref,
                       gn_ref, *rest):
    o_ref, so_ref, s_ref = rest[-3:]
    s_ref[...] = jnp.zeros_like(s_ref)

    def body(c, carry):
        st = pl.multiple_of(c * RET_L, RET_L)
        rows = pl.ds(st, RET_L)
        y, s_new = _retention_block(
            q_ref[rows, :].astype(F32), k_ref[rows, :].astype(F32), v_ref[rows, :],
            g_ref[rows, :].astype(F32), cos_ref[rows, :], sin_ref[rows, :], s_ref[...],
            dm_ref[...], qd_ref[...], kd_ref[...], sd_ref[...], gn_ref[...])
        o_ref[rows, :] = y.astype(o_ref.dtype)
        s_ref[...] = s_new
        return carry

    lax.fori_loop(0, TP // RET_L, body, 0)
    so_ref[...] = s_ref[...]


def _head_const_specs(block):
    return [pl.BlockSpec((None, block, block), lambda b, hd: (hd, 0, 0)),
            pl.BlockSpec((None, block, 1), lambda b, hd: (hd, 0, 0)),
            pl.BlockSpec((None, block, 1), lambda b, hd: (hd, 0, 0)),
            pl.BlockSpec((None, 1, 1), lambda b, hd: (hd, 0, 0))]


def _state_stack(batch, stack):
    shape = jax.ShapeDtypeStruct((DEPTH, batch, RET_HEADS, RET_DK, RET_DV), F32)
    extra_specs = [] if stack is None else [pl.BlockSpec(memory_space=pl.ANY)]
    extra_args = [] if stack is None else [stack]
    return shape, extra_specs, extra_args


def _ret_prompt(h, cos, sin, consts, gn_w, layer, stack):
    def col_spec(off, width):
        return pl.BlockSpec((TP, width), lambda b, hd: (b, off // width + hd))

    table = pl.BlockSpec((TP, RET_DK // 2), lambda b, hd: (0, 0))
    stack_shape, extra_specs, extra_args = _state_stack(BP, stack)
    return pl.pallas_call(
        _ret_prompt_kernel,
        out_shape=(jax.ShapeDtypeStruct((M_ALL, RET_V_WIDTH), BF16), stack_shape),
        grid=(BP, RET_HEADS),
        in_specs=[col_spec(QR, RET_DK), col_spec(KR, RET_DK), col_spec(VR, RET_DV), col_spec(GR, RET_DV),
                  table, table, *_head_const_specs(RET_L),
                  pl.BlockSpec((None, 1, RET_DV), lambda b, hd: (layer, 0, hd)), *extra_specs],
        out_specs=(pl.BlockSpec((TP, RET_DV), lambda b, hd: (b, hd)),
                   pl.BlockSpec((None, None, None, RET_DK, RET_DV), lambda b, hd: (layer, b, hd, 0, 0))),
        scratch_shapes=[pltpu.VMEM((RET_DK, RET_DV), F32)],
        input_output_aliases={} if stack is None else {11: 1},
        compiler_params=_params(2),
        name="retention_prompt",
    )(h, h, h, h, cos, sin, *consts, gn_w, *extra_args)


def _pad_rows(x, rows):
    return jnp.concatenate([x, jnp.zeros((rows - x.shape[0], x.shape[1]), x.dtype)], axis=0)


def _ret_sample_kernel(q_ref, k_ref, v_ref, g_ref, cos_ref, sin_ref, dm_ref, qd_ref, kd_ref, sd_ref,
                       gn_ref, s0_ref, *rest):
    o_ref, so_ref = rest[-2:]
    y, s_new = _retention_block(
        _pad_rows(q_ref[...].astype(F32), SAMPLE_L), _pad_rows(k_ref[...].astype(F32), SAMPLE_L),
        _pad_rows(v_ref[...].astype(F32), SAMPLE_L).astype(BF16),
        _pad_rows(g_ref[...].astype(F32), SAMPLE_L), cos_ref[...], sin_ref[...], s0_ref[...],
        dm_ref[...], qd_ref[...], kd_ref[...], sd_ref[...], gn_ref[...])
    o_ref[...] = y[:TS].astype(o_ref.dtype)
    so_ref[...] = s_new


def _ret_sample(h, cos, sin, consts, gn_w, state, layer, yr, stack):
    row0 = MP // TS

    def col_spec(off, width):
        return pl.BlockSpec((TS, width), lambda b, hd: (row0 + b, off // width + hd))

    table = pl.BlockSpec((SAMPLE_L, RET_DK // 2), lambda b, hd: (0, 0))
    state_spec = pl.BlockSpec((None, None, None, RET_DK, RET_DV), lambda b, hd: (layer, b, hd, 0, 0))
    stack_shape, extra_specs, extra_args = _state_stack(BS, stack)
    return pl.pallas_call(
        _ret_sample_kernel,
        out_shape=(jax.ShapeDtypeStruct(yr.shape, yr.dtype), stack_shape),
        grid=(BS, RET_HEADS),
        in_specs=[col_spec(QR, RET_DK), col_spec(KR, RET_DK), col_spec(VR, RET_DV), col_spec(GR, RET_DV),
                  table, table, *_head_const_specs(SAMPLE_L),
                  pl.BlockSpec((None, 1, RET_DV), lambda b, hd: (layer, 0, hd)),
                  state_spec, pl.BlockSpec(memory_space=pl.ANY), *extra_specs],
        out_specs=(pl.BlockSpec((TS, RET_DV), lambda b, hd: (row0 + b, hd)), state_spec),
        input_output_aliases={12: 0} if stack is None else {12: 0, 13: 1},
        compiler_params=_params(2),
        name="retention_sample",
    )(h, h, h, h, cos, sin, *consts, gn_w, state, yr, *extra_args)


def _conv_kernel(*refs, tt, aliased):
    if aliased:
        (ca_ref, cb_ref, hist_ref, w_ref, b_ref, lnw_ref, lnb_ref, _, o_ref, tail_ref,
         u_ref, y_ref, sh_ref) = refs
    else:
        (ca_ref, cb_ref, hist_ref, w_ref, b_ref, lnw_ref, lnb_ref, o_ref, tail_ref,
         u_ref, y_ref, sh_ref) = refs
    t = pl.program_id(1)

    @pl.when(t == 0)
    def _():
        u_ref[pl.ds(0, CONV_HIST), :] = hist_ref[...]

    u_ref[pl.ds(CONV_HIST, tt), :] = ca_ref[...].astype(F32) * _sigmoid(cb_ref[...].astype(F32))

    lanes = CONV_LANES
    n_win = 0
    for c0 in range(0, CONV_C, lanes):
        acc = jnp.zeros((tt, lanes), F32)
        for b in range(8):
            taps = [k for k in range(b, CONV_W, 8)]
            rows = tt + 8 * (len(taps) - 1)
            sh = sh_ref.at[n_win % 2]
            n_win += 1
            sh[pl.ds(0, rows), :] = u_ref[pl.ds(2 + b, rows), pl.ds(c0, lanes)]
            for a, k in enumerate(taps):
                acc = acc + sh[pl.ds(8 * a, tt), :] * w_ref[pl.ds(k, 1), pl.ds(c0, lanes)]
        y_ref[:, pl.ds(c0, lanes)] = acc

    y = y_ref[...] + b_ref[...]
    mu = jnp.mean(y, axis=-1, keepdims=True)
    var = jnp.mean(jnp.square(y - mu), axis=-1, keepdims=True)
    y = (y - mu) * lax.rsqrt(var + 1e-5) * lnw_ref[...] + lnb_ref[...]
    o_ref[...] = _silu(y).astype(o_ref.dtype)

    tail = u_ref[pl.ds(tt, CONV_HIST), :]
    tail_ref[...] = tail
    u_ref[pl.ds(0, CONV_HIST), :] = tail


def _conv(h, hist, dw_w, dw_b, ln_w, ln_b, layer, *, batch, steps, tt, row0, yc=None):
    def col_spec(off):
        return pl.BlockSpec((tt, CONV_C), lambda b, t: (row0 + b * steps + t, off // CONV_C))

    def vec_spec():
        return pl.BlockSpec((None, 1, CONV_C), lambda b, t: (layer, 0, 0))

    in_specs = [col_spec(CA), col_spec(CB),
                pl.BlockSpec((None, CONV_HIST, CONV_C), lambda b, t: (b, 0, 0)),
                pl.BlockSpec((None, 32, CONV_C), lambda b, t: (layer, 0, 0)),
                vec_spec(), vec_spec(), vec_spec()]
    args = [h, h, hist, dw_w, dw_b, ln_w, ln_b]
    aliases = {}
    if yc is not None:
        in_specs.append(pl.BlockSpec(memory_space=pl.ANY))
        args.append(yc)
        aliases = {7: 0}
    return pl.pallas_call(
        functools.partial(_conv_kernel, tt=tt, aliased=yc is not None),
        out_shape=(jax.ShapeDtypeStruct((M_ALL, CONV_C), BF16),
                   jax.ShapeDtypeStruct((batch, CONV_HIST, CONV_C), F32)),
        grid=(batch, steps),
        in_specs=in_specs,
        out_specs=(pl.BlockSpec((tt, CONV_C), lambda b, t: (row0 + b * steps + t, 0)),
                   pl.BlockSpec((None, CONV_HIST, CONV_C), lambda b, t: (b, 0, 0))),
        scratch_shapes=[pltpu.VMEM((CONV_HIST + tt, CONV_C), F32), pltpu.VMEM((tt, CONV_C), F32),
                        pltpu.VMEM((2, tt + 24, CONV_LANES), F32)],
        input_output_aliases=aliases,
        compiler_params=_params(2),
        name="conv_prompt" if yc is None else "conv_sample",
    )(*args)


def kernel(x_prompt, x_sample, cache_attn_k, cache_attn_v, state_ret, cache_conv, norm_mix_pre, norm_mix_post, norm_ffn_pre, norm_ffn_post, w_in, attn_rel_bias, ret_gn_w, conv_dw_w, conv_dw_b, conv_ln_w, conv_ln_b, w_branch_attn, w_branch_ret, w_branch_conv, w_out, w_ffn_gate, w_ffn_up, w_ffn_down):
    x = jnp.concatenate([x_prompt.reshape(MP, D_MODEL), x_sample.reshape(BS * TS, D_MODEL)], axis=0)

    def row_vec(w):
        return w.reshape(DEPTH, 1, w.shape[-1])

    n_mp, n_mq, n_fp, n_fq = map(row_vec, (norm_mix_pre, norm_mix_post, norm_ffn_pre, norm_ffn_post))
    gn_w, dw_b, ln_w, ln_b = map(row_vec, (ret_gn_w, conv_dw_b, conv_ln_w, conv_ln_b))
    dw_w = jnp.pad(conv_dw_w, ((0, 0), (0, 32 - CONV_W), (0, 0)))
    hist_p = jnp.zeros((BP, CONV_HIST, CONV_C), F32)
    hist_s = jnp.pad(cache_conv, ((0, 0), (0, 0), (CONV_HIST - (CONV_W - 1), 0), (0, 0)))

    bias_p, bias_s = _rel_bias_tiles(attn_rel_bias)
    cos_p, sin_p = _rope_tables(jnp.arange(TP))
    cos_s, sin_s = _rope_tables(PAST_LEN + jnp.arange(TS))
    cos_s = jnp.pad(cos_s, ((0, SAMPLE_L - TS), (0, 0)))
    sin_s = jnp.pad(sin_s, ((0, SAMPLE_L - TS), (0, 0)))
    ret_p = _retention_consts(RET_L, RET_L)
    ret_s = _retention_consts(SAMPLE_L, TS)

    kp_l, vp_l, cp_l, ks_l, vs_l, cs_l = ([] for _ in range(6))
    s_p = s_s = None
    xn = _pre_norm(x, n_mp, 0)
    for l in range(DEPTH):
        h = _proj(xn, w_in, l, tn=1024, tm=TM, name="in_proj")

        ya = _attn_prompt(h, bias_p, l)
        ya = _attn_sample(h, cache_attn_k, cache_attn_v, bias_s, l, ya)
        yr, s_p = _ret_prompt(h, cos_p, sin_p, ret_p, gn_w, l, s_p)
        yr, s_s = _ret_sample(h, cos_s, sin_s, ret_s, gn_w, state_ret, l, yr, s_s)
        yc, tail_p = _conv(h, hist_p, dw_w, dw_b, ln_w, ln_b, l,
                           batch=BP, steps=TP // CONV_TT, tt=CONV_TT, row0=0)
        yc, tail_s = _conv(h, hist_s[l], dw_w, dw_b, ln_w, ln_b, l,
                           batch=BS, steps=1, tt=TS, row0=MP // TS, yc=yc)

        merged = _merge(ya, yr, yc, h, w_branch_attn, w_branch_ret, w_branch_conv, l)
        y = _proj(merged, w_out, l, tn=1024, tm=TM, name="out_proj")
        x, xn = _post_norm(y, x, n_mq, l, n_fp, l)

        hidden = _swiglu_up(xn, w_ffn_gate, w_ffn_up, l)
        y = _proj(hidden, w_ffn_down, l, tn=512, tm=TM // 2, name="ffn_down")
        x, xn = _post_norm(y, x, n_fq, l, n_mp, min(l + 1, DEPTH - 1))

        kv_p = jnp.stack([h[(b + 1) * TP - ATT_BAND:(b + 1) * TP, KA:VA + ATT_WIDTH] for b in range(BP)])
        kv_s = h[MP:, KA:VA + ATT_WIDTH].reshape(BS, TS, 2 * ATT_WIDTH)
        kp_l.append(kv_p[:, :, :ATT_WIDTH].astype(F32).reshape(BP, ATT_BAND, ATT_HEADS, ATT_DH))
        vp_l.append(kv_p[:, :, ATT_WIDTH:].astype(F32).reshape(BP, ATT_BAND, ATT_HEADS, ATT_DH))
        ks_l.append(kv_s[:, :, :ATT_WIDTH].astype(F32).reshape(BS, TS, ATT_HEADS, ATT_DH))
        vs_l.append(kv_s[:, :, ATT_WIDTH:].astype(F32).reshape(BS, TS, ATT_HEADS, ATT_DH))
        cp_l.append(tail_p[:, CONV_HIST - (CONV_W - 1):])
        cs_l.append(tail_s[:, CONV_HIST - (CONV_W - 1):])

    return (x[:MP].reshape(BP, TP, D_MODEL), x[MP:].reshape(BS, TS, D_MODEL),
            jnp.stack(kp_l), jnp.stack(vp_l), s_p, jnp.stack(cp_l),
            jnp.stack(ks_l), jnp.stack(vs_l), s_s, jnp.stack(cs_l))
```

```python
import functools

import jax
import jax.numpy as jnp
from jax import lax
from jax.experimental import pallas as pl
from jax.experimental.pallas import tpu as pltpu

F32 = jnp.float32
BF16 = jnp.bfloat16

D_MODEL = 4096
DEPTH = 4
BP, TP = 4, 2048
BS, TS = 8, 16
PAST_LEN = 2048
MP = BP * TP
M_ALL = MP + BS * TS

CHUNK = 64
ATT_HEADS, ATT_DH = 16, 128
ATT_WIDTH = ATT_HEADS * ATT_DH
ATT_BAND = 8 * CHUNK
REL_CLIP = 128
RET_HEADS, RET_DK, RET_DV = 8, 256, 512
RET_V_WIDTH = RET_HEADS * RET_DV
ROPE_BASE = 10000.0
CONV_C, CONV_W = 2048, 31
FFN_HIDDEN = 11008
NEG_INF = -1e30

QA, KA, VA = 0, 2048, 4096
QR, KR, VR, GR = 6144, 8192, 10240, 14336
CA, CB = 18432, 20480
GA, GB, GC = 22528, 26624, 30720
N_IN = 34816

VMEM_LIMIT_BYTES = 56 * 1024 * 1024

TM = 832
W_CHUNKS = 8
RET_L = 256
ATT_G = 256
ATT_KW = ATT_G + ATT_BAND
SAMPLE_KW = 640
SAMPLE_L = 128
CONV_TT = 64
CONV_HIST = 32
CONV_LANES = 256


def _params(n_axes):
    return pltpu.CompilerParams(dimension_semantics=("arbitrary",) * n_axes,
                                vmem_limit_bytes=VMEM_LIMIT_BYTES)


def _sigmoid(x):
    return 1.0 / (1.0 + jnp.exp(-x))


def _silu(x):
    return x * _sigmoid(x)


def _tok_block(jj, s, steps):
    return jnp.minimum(s, jj * steps)


def _col_block(jj):
    return jnp.maximum(jj - 1, 0)


def _weight_spec(layer, k, tn, n_tiles):
    def index(jj, s):
        chunk = jnp.where(jj >= n_tiles, W_CHUNKS - 1, jnp.minimum(s, W_CHUNKS - 1))
        return (layer, chunk, jnp.minimum(jj, n_tiles - 1))
    return pl.BlockSpec((None, k // W_CHUNKS, tn), index)


def _stage_weights(w_refs, wb_refs, n_tiles):
    jj, s = pl.program_id(0), pl.program_id(1)

    @pl.when((s < W_CHUNKS) & (jj < n_tiles))
    def _():
        for w_ref, wb_ref in zip(w_refs, wb_refs):
            rows = w_ref.shape[0]
            r = pl.multiple_of(s * rows, 16)
            wb_ref[jj % 2, pl.ds(r, rows), :] = w_ref[...].astype(BF16)


def _proj_kernel(x_ref, w_ref, o_ref, wb_ref, *, n_tiles):
    _stage_weights([w_ref], [wb_ref], n_tiles)
    jj = pl.program_id(0)

    @pl.when(jj > 0)
    def _():
        w = wb_ref[(jj + 1) % 2]
        o_ref[...] = jnp.dot(x_ref[...], w, preferred_element_type=F32).astype(o_ref.dtype)


def _proj(x, w, layer, *, tn, tm, name):
    m, k = x.shape
    n = w.shape[2]
    n_tiles, steps = n // tn, m // tm
    assert steps >= W_CHUNKS and k % (16 * W_CHUNKS) == 0 and n % tn == 0 and m % tm == 0
    return pl.pallas_call(
        functools.partial(_proj_kernel, n_tiles=n_tiles),
        out_shape=jax.ShapeDtypeStruct((m, n), BF16),
        grid=(n_tiles + 1, steps),
        in_specs=[pl.BlockSpec((tm, k), lambda jj, s: (_tok_block(jj, s, steps), 0)),
                  _weight_spec(layer, k, tn, n_tiles)],
        out_specs=pl.BlockSpec((tm, tn), lambda jj, s: (_tok_block(jj, s, steps), _col_block(jj))),
        scratch_shapes=[pltpu.VMEM((2, k, tn), BF16)],
        compiler_params=_params(2),
        name=name,
    )(x, w)


def _swiglu_kernel(x_ref, wg_ref, wu_ref, o_ref, wgb_ref, wub_ref, *, n_tiles):
    _stage_weights([wg_ref, wu_ref], [wgb_ref, wub_ref], n_tiles)
    jj = pl.program_id(0)

    @pl.when(jj > 0)
    def _():
        slot = (jj + 1) % 2
        x = x_ref[...]
        a = jnp.dot(x, wgb_ref[slot], preferred_element_type=F32)
        b = jnp.dot(x, wub_ref[slot], preferred_element_type=F32)
        o_ref[...] = (_silu(a) * b).astype(o_ref.dtype)


def _swiglu_up(xn, w_gate, w_up, layer):
    tn = 256
    m = xn.shape[0]
    n_tiles, steps = FFN_HIDDEN // tn, m // TM
    assert steps >= W_CHUNKS and FFN_HIDDEN % tn == 0 and m % TM == 0
    return pl.pallas_call(
        functools.partial(_swiglu_kernel, n_tiles=n_tiles),
        out_shape=jax.ShapeDtypeStruct((m, FFN_HIDDEN), BF16),
        grid=(n_tiles + 1, steps),
        in_specs=[pl.BlockSpec((TM, D_MODEL), lambda jj, s: (_tok_block(jj, s, steps), 0)),
                  _weight_spec(layer, D_MODEL, tn, n_tiles), _weight_spec(layer, D_MODEL, tn, n_tiles)],
        out_specs=pl.BlockSpec((TM, tn), lambda jj, s: (_tok_block(jj, s, steps), _col_block(jj))),
        scratch_shapes=[pltpu.VMEM((2, D_MODEL, tn), BF16), pltpu.VMEM((2, D_MODEL, tn), BF16)],
        compiler_params=_params(2),
        name="ffn_gate_up",
    )(xn, w_gate, w_up)


def _merge_kernel(ya_ref, yr_ref, yc_ref, ga_ref, gb_ref, gc_ref, wa_ref, wr_ref, wc_ref,
                  o_ref, wab_ref, wrb_ref, wcb_ref, *, n_tiles):
    _stage_weights([wa_ref, wr_ref, wc_ref], [wab_ref, wrb_ref, wcb_ref], n_tiles)
    jj = pl.program_id(0)

    @pl.when(jj > 0)
    def _():
        slot = (jj + 1) % 2
        a = jnp.dot(ya_ref[...], wab_ref[slot], preferred_element_type=F32)
        r = jnp.dot(yr_ref[...], wrb_ref[slot], preferred_element_type=F32)
        c = jnp.dot(yc_ref[...], wcb_ref[slot], preferred_element_type=F32)
        out = (_sigmoid(ga_ref[...].astype(F32)) * a + _sigmoid(gb_ref[...].astype(F32)) * r
               + _sigmoid(gc_ref[...].astype(F32)) * c)
        o_ref[...] = out.astype(o_ref.dtype)


def _merge(ya, yr, yc, h, w_ba, w_br, w_bc, layer):
    tn = 512
    tm = 416
    m = ya.shape[0]
    n_tiles, steps = D_MODEL // tn, m // tm
    assert steps >= W_CHUNKS and m % tm == 0

    def gate_spec(off):
        return pl.BlockSpec((tm, tn),
                            lambda jj, s: (_tok_block(jj, s, steps), off // tn + _col_block(jj)))

    def x_spec(k):
        return pl.BlockSpec((tm, k), lambda jj, s: (_tok_block(jj, s, steps), 0))

    return pl.pallas_call(
        functools.partial(_merge_kernel, n_tiles=n_tiles),
        out_shape=jax.ShapeDtypeStruct((m, D_MODEL), BF16),
        grid=(n_tiles + 1, steps),
        in_specs=[x_spec(ATT_WIDTH), x_spec(RET_V_WIDTH), x_spec(CONV_C),
                  gate_spec(GA), gate_spec(GB), gate_spec(GC),
                  _weight_spec(layer, ATT_WIDTH, tn, n_tiles),
                  _weight_spec(layer, RET_V_WIDTH, tn, n_tiles),
                  _weight_spec(layer, CONV_C, tn, n_tiles)],
        out_specs=pl.BlockSpec((tm, tn), lambda jj, s: (_tok_block(jj, s, steps), _col_block(jj))),
        scratch_shapes=[pltpu.VMEM((2, ATT_WIDTH, tn), BF16), pltpu.VMEM((2, RET_V_WIDTH, tn), BF16),
                        pltpu.VMEM((2, CONV_C, tn), BF16)],
        compiler_params=_params(2),
        name="branch_merge",
    )(ya, yr, yc, h, h, h, w_ba, w_br, w_bc)


def _rms(x, w, eps=1e-6):
    return x * lax.rsqrt(jnp.mean(x * x, axis=-1, keepdims=True) + eps) * w


def _pre_norm_kernel(x_ref, w_ref, o_ref):
    o_ref[...] = _rms(x_ref[...], w_ref[...]).astype(o_ref.dtype)


def _pre_norm(x, w, layer):
    tm = 208
    m = x.shape[0]
    return pl.pallas_call(
        _pre_norm_kernel,
        out_shape=jax.ShapeDtypeStruct((m, D_MODEL), BF16),
        grid=(m // tm,),
        in_specs=[pl.BlockSpec((tm, D_MODEL), lambda i: (i, 0)),
                  pl.BlockSpec((None, 1, D_MODEL), lambda i: (layer, 0, 0))],
        out_specs=pl.BlockSpec((tm, D_MODEL), lambda i: (i, 0)),
        compiler_params=_params(1),
        name="pre_norm",
    )(x, w)


def _post_norm_kernel(y_ref, x_ref, wp_ref, wn_ref, xo_ref, xn_ref):
    x_new = x_ref[...] + _rms(y_ref[...].astype(F32), wp_ref[...])
    xo_ref[...] = x_new
    xn_ref[...] = _rms(x_new, wn_ref[...]).astype(xn_ref.dtype)


def _post_norm(y, x, w_post, layer, w_next, next_layer):
    tm = 208
    m = x.shape[0]
    row = pl.BlockSpec((tm, D_MODEL), lambda i: (i, 0))
    return pl.pallas_call(
        _post_norm_kernel,
        out_shape=(jax.ShapeDtypeStruct((m, D_MODEL), F32),
                   jax.ShapeDtypeStruct((m, D_MODEL), BF16)),
        grid=(m // tm,),
        in_specs=[row, row,
                  pl.BlockSpec((None, 1, D_MODEL), lambda i: (layer, 0, 0)),
                  pl.BlockSpec((None, 1, D_MODEL), lambda i: (next_layer, 0, 0))],
        out_specs=(row, row),
        compiler_params=_params(1),
        name="post_norm",
    )(y, x, w_post, w_next)


def _bias_kernel(t_ref, bp_ref, bs_ref):
    t = t_ref[...]
    b1 = t.astype(BF16)
    r1 = t - b1.astype(F32)
    b2 = r1.astype(BF16)
    b3 = (r1 - b2.astype(F32)).astype(BF16)
    width = ATT_KW + ATT_G
    r = lax.broadcasted_iota(jnp.int32, (384, width), 0)
    mm = lax.broadcasted_iota(jnp.int32, (384, width), 1)
    idx = jnp.clip(ATT_BAND + ATT_G - mm, -REL_CLIP, REL_CLIP) + REL_CLIP
    onehot = jnp.where(idx == r, 1.0, 0.0).astype(BF16)
    g = (jnp.dot(b1, onehot, preferred_element_type=F32)
         + jnp.dot(b2, onehot, preferred_element_type=F32)
         + jnp.dot(b3, onehot, preferred_element_type=F32))
    i = lax.broadcasted_iota(jnp.int32, (ATT_G, ATT_KW), 0)
    j = lax.broadcasted_iota(jnp.int32, (ATT_G, ATT_KW), 1)
    rel = jnp.right_shift(i, 6) + 8 - jnp.right_shift(j, 6)
    valid = (rel >= 0) & (rel <= 8)
    js = lax.broadcasted_iota(jnp.int32, (TS, SAMPLE_KW), 1)
    for hd in range(ATT_HEADS):
        gb = jnp.broadcast_to(g[hd:hd + 1, :], (ATT_G, width))
        y = pltpu.roll(gb, ATT_KW, 1, stride=1, stride_axis=0)
        bp_ref[hd] = jnp.where(valid, y[:, :ATT_KW], NEG_INF)
        bs_ref[hd] = jnp.where(js < ATT_BAND + TS, y[:TS, :SAMPLE_KW], NEG_INF)


def _rel_bias_tiles(table):
    tt = jnp.transpose(table, (0, 2, 1))
    tt = jnp.pad(tt, ((0, 0), (0, 0), (0, 384 - tt.shape[2])))
    return pl.pallas_call(
        _bias_kernel,
        out_shape=(jax.ShapeDtypeStruct((DEPTH, ATT_HEADS, ATT_G, ATT_KW), F32),
                   jax.ShapeDtypeStruct((DEPTH, ATT_HEADS, TS, SAMPLE_KW), F32)),
        grid=(DEPTH,),
        in_specs=[pl.BlockSpec((None, ATT_HEADS, 384), lambda l: (l, 0, 0))],
        out_specs=(pl.BlockSpec((None, ATT_HEADS, ATT_G, ATT_KW), lambda l: (l, 0, 0, 0)),
                   pl.BlockSpec((None, ATT_HEADS, TS, SAMPLE_KW), lambda l: (l, 0, 0, 0))),
        compiler_params=_params(1),
        name="rel_bias_tiles",
    )(tt)


def _softmax_pv(s, v):
    m = jnp.max(s, axis=-1, keepdims=True)
    p = jnp.exp(s - m)
    denom = jnp.sum(p, axis=-1, keepdims=True)
    o = jnp.dot(p.astype(BF16), v, preferred_element_type=F32)
    return o / denom


def _attn_prompt_kernel(q_ref, k_ref, v_ref, b_ref, o_ref, kp_ref, vp_ref):
    zeros = jnp.zeros((ATT_BAND, ATT_DH), BF16)
    kp_ref[pl.ds(0, ATT_BAND), :] = zeros
    vp_ref[pl.ds(0, ATT_BAND), :] = zeros
    kp_ref[pl.ds(ATT_BAND, TP), :] = k_ref[...]
    vp_ref[pl.ds(ATT_BAND, TP), :] = v_ref[...]
    col = lax.broadcasted_iota(jnp.int32, (ATT_G, ATT_KW), 1)

    def body(g, carry):
        start = pl.multiple_of(g * ATT_G, ATT_G)
        q = q_ref[pl.ds(start, ATT_G), :]
        kb = kp_ref[pl.ds(start, ATT_KW), :]
        vb = vp_ref[pl.ds(start, ATT_KW), :]
        s = lax.dot_general(q, kb, (((1,), (1,)), ((), ())), preferred_element_type=F32)
        s = s * (ATT_DH ** -0.5) + b_ref[...]
        s = jnp.where(col >= ATT_BAND - start, s, NEG_INF)
        o_ref[pl.ds(start, ATT_G), :] = _softmax_pv(s, vb).astype(o_ref.dtype)
        return carry

    lax.fori_loop(0, TP // ATT_G, body, 0, unroll=4)


def _attn_prompt(h, bias_p, layer):
    def col_spec(off):
        return pl.BlockSpec((TP, ATT_DH), lambda b, hd: (b, off // ATT_DH + hd))

    return pl.pallas_call(
        _attn_prompt_kernel,
        out_shape=jax.ShapeDtypeStruct((M_ALL, ATT_WIDTH), BF16),
        grid=(BP, ATT_HEADS),
        in_specs=[col_spec(QA), col_spec(KA), col_spec(VA),
                  pl.BlockSpec((None, None, ATT_G, ATT_KW), lambda b, hd: (layer, hd, 0, 0))],
        out_specs=pl.BlockSpec((TP, ATT_DH), lambda b, hd: (b, hd)),
        scratch_shapes=[pltpu.VMEM((ATT_BAND + TP, ATT_DH), BF16),
                        pltpu.VMEM((ATT_BAND + TP, ATT_DH), BF16)],
        compiler_params=_params(2),
        name="attn_prompt",
    )(h, h, h, bias_p)


def _attn_sample_kernel(q_ref, k_ref, v_ref, ck_ref, cv_ref, b_ref, ya_hbm, o_ref, kk_ref, vv_ref):
    del ya_hbm
    n_pad = SAMPLE_KW - ATT_BAND - TS
    pad = jnp.zeros((n_pad, ATT_DH), BF16)
    kk_ref[pl.ds(ATT_BAND + TS, n_pad), :] = pad
    vv_ref[pl.ds(ATT_BAND + TS, n_pad), :] = pad
    for hd in range(ATT_HEADS):
        cols = pl.ds(hd * ATT_DH, ATT_DH)
        kk_ref[pl.ds(0, ATT_BAND), :] = ck_ref[:, hd, :].astype(BF16)
        vv_ref[pl.ds(0, ATT_BAND), :] = cv_ref[:, hd, :].astype(BF16)
        kk_ref[pl.ds(ATT_BAND, TS), :] = k_ref[:, cols]
        vv_ref[pl.ds(ATT_BAND, TS), :] = v_ref[:, cols]
        s = lax.dot_general(q_ref[:, cols], kk_ref[...], (((1,), (1,)), ((), ())),
                            preferred_element_type=F32)
        s = s * (ATT_DH ** -0.5) + b_ref[hd]
        o_ref[:, cols] = _softmax_pv(s, vv_ref[...]).astype(o_ref.dtype)


def _attn_sample(h, cache_k, cache_v, bias_s, layer, ya):
    row0 = MP // TS

    def col_spec(off):
        return pl.BlockSpec((TS, ATT_WIDTH), lambda b: (row0 + b, off // ATT_WIDTH))

    cache_spec = pl.BlockSpec((None, None, ATT_BAND, ATT_HEADS, ATT_DH), lambda b: (layer, b, 0, 0, 0))
    return pl.pallas_call(
        _attn_sample_kernel,
        out_shape=jax.ShapeDtypeStruct(ya.shape, ya.dtype),
        grid=(BS,),
        in_specs=[col_spec(QA), col_spec(KA), col_spec(VA), cache_spec, cache_spec,
                  pl.BlockSpec((None, ATT_HEADS, TS, SAMPLE_KW), lambda b: (layer, 0, 0, 0)),
                  pl.BlockSpec(memory_space=pl.ANY)],
        out_specs=pl.BlockSpec((TS, ATT_WIDTH), lambda b: (row0 + b, 0)),
        scratch_shapes=[pltpu.VMEM((SAMPLE_KW, ATT_DH), BF16), pltpu.VMEM((SAMPLE_KW, ATT_DH), BF16)],
        input_output_aliases={6: 0},
        compiler_params=_params(1),
        name="attn_sample",
    )(h, h, h, cache_k, cache_v, bias_s, ya)


def _retention_consts(block, length):
    lg = jnp.log1p(-jnp.exp2(-5.0 - jnp.arange(RET_HEADS, dtype=F32)))
    i = jnp.arange(block, dtype=F32)
    diff = i[:, None] - i[None, :]
    dmat = jnp.where(diff >= 0, jnp.exp(lg[:, None, None] * jnp.maximum(diff, 0.0)), 0.0)
    q_decay = jnp.exp(lg[:, None] * (i[None, :] + 1.0))[:, :, None]
    k_decay = jnp.where(i[None, :] < length,
                        jnp.exp(lg[:, None] * jnp.maximum(length - 1.0 - i[None, :], 0.0)), 0.0)[:, :, None]
    s_decay = jnp.exp(lg * length)[:, None, None]
    return dmat, q_decay, k_decay, s_decay


def _rope_tables(pos):
    half = RET_DK // 2
    inv = 1.0 / (ROPE_BASE ** (jnp.arange(half, dtype=F32) / half))
    ang = pos.astype(F32)[:, None] * inv[None, :]
    return jnp.cos(ang), jnp.sin(ang)


def _rotary(x, cos, sin):
    half = RET_DK // 2
    x1, x2 = x[:, :half], x[:, half:]
    return jnp.concatenate([x1 * cos - x2 * sin, x1 * sin + x2 * cos], axis=-1)


def _retention_block(q, k, v, g, cos, sin, s, dmat, q_decay, k_decay, s_decay, gn_w):
    qb = _rotary(q, cos, sin).astype(BF16)
    kr = _rotary(k, cos, sin) * (RET_DK ** -0.5)
    scores = lax.dot_general(qb, kr.astype(BF16), (((1,), (1,)), ((), ())),
                             preferred_element_type=F32) * dmat
    o = (jnp.dot(scores.astype(BF16), v, preferred_element_type=F32)
         + jnp.dot(qb, s.astype(BF16), preferred_element_type=F32) * q_decay)
    kt = jnp.transpose(kr * k_decay).astype(BF16)
    s_new = s_decay * s + jnp.dot(kt, v, preferred_element_type=F32)
    mu = jnp.mean(o, axis=-1, keepdims=True)
    var = jnp.mean(jnp.square(o - mu), axis=-1, keepdims=True)
    y = (o - mu) * lax.rsqrt(var + 1e-5) * gn_w * _silu(g)
    return y, s_new


def _ret_prompt_kernel(q_ref, k_ref, v_ref, g_ref, cos_ref, sin_ref, dm_ref, qd_ref, kd_ref, sd_ref,
                       gn_ref, *rest):
    o_ref, so_ref, s_ref = rest[-3:]
    s_ref[...] = jnp.zeros_like(s_ref)

    def body(c, carry):
        st = pl.multiple_of(c * RET_L, RET_L)
        rows = pl.ds(st, RET_L)
        y, s_new = _retention_block(
            q_ref[rows, :].astype(F32), k_ref[rows, :].astype(F32), v_ref[rows, :],
            g_ref[rows, :].astype(F32), cos_ref[rows, :], sin_ref[rows, :], s_ref[...],
            dm_ref[...], qd_ref[...], kd_ref[...], sd_ref[...], gn_ref[...])
        o_ref[rows, :] = y.astype(o_ref.dtype)
        s_ref[...] = s_new
        return carry

    lax.fori_loop(0, TP // RET_L, body, 0, unroll=2)
    so_ref[...] = s_ref[...]


def _head_const_specs(block):
    return [pl.BlockSpec((None, block, block), lambda b, hd: (hd, 0, 0)),
            pl.BlockSpec((None, block, 1), lambda b, hd: (hd, 0, 0)),
            pl.BlockSpec((None, block, 1), lambda b, hd: (hd, 0, 0)),
            pl.BlockSpec((None, 1, 1), lambda b, hd: (hd, 0, 0))]


def _state_stack(batch, stack):
    shape = jax.ShapeDtypeStruct((DEPTH, batch, RET_HEADS, RET_DK, RET_DV), F32)
    extra_specs = [] if stack is None else [pl.BlockSpec(memory_space=pl.ANY)]
    extra_args = [] if stack is None else [stack]
    return shape, extra_specs, extra_args


def _ret_prompt(h, cos, sin, consts, gn_w, layer, stack):
    def col_spec(off, width):
        return pl.BlockSpec((TP, width), lambda b, hd: (b, off // width + hd))

    table = pl.BlockSpec((TP, RET_DK // 2), lambda b, hd: (0, 0))
    stack_shape, extra_specs, extra_args = _state_stack(BP, stack)
    return pl.pallas_call(
        _ret_prompt_kernel,
        out_shape=(jax.ShapeDtypeStruct((M_ALL, RET_V_WIDTH), BF16), stack_shape),
        grid=(BP, RET_HEADS),
        in_specs=[col_spec(QR, RET_DK), col_spec(KR, RET_DK), col_spec(VR, RET_DV), col_spec(GR, RET_DV),
                  table, table, *_head_const_specs(RET_L),
                  pl.BlockSpec((None, 1, RET_DV), lambda b, hd: (layer, 0, hd)), *extra_specs],
        out_specs=(pl.BlockSpec((TP, RET_DV), lambda b, hd: (b, hd)),
                   pl.BlockSpec((None, None, None, RET_DK, RET_DV), lambda b, hd: (layer, b, hd, 0, 0))),
        scratch_shapes=[pltpu.VMEM((RET_DK, RET_DV), F32)],
        input_output_aliases={} if stack is None else {11: 1},
        compiler_params=_params(2),
        name="retention_prompt",
    )(h, h, h, h, cos, sin, *consts, gn_w, *extra_args)


def _pad_rows(x, rows):
    return jnp.concatenate([x, jnp.zeros((rows - x.shape[0], x.shape[1]), x.dtype)], axis=0)


def _ret_sample_kernel(q_ref, k_ref, v_ref, g_ref, cos_ref, sin_ref, dm_ref, qd_ref, kd_ref, sd_ref,
                       gn_ref, s0_ref, *rest):
    o_ref, so_ref = rest[-2:]
    y, s_new = _retention_block(
        _pad_rows(q_ref[...].astype(F32), SAMPLE_L), _pad_rows(k_ref[...].astype(F32), SAMPLE_L),
        _pad_rows(v_ref[...].astype(F32), SAMPLE_L).astype(BF16),
        _pad_rows(g_ref[...].astype(F32), SAMPLE_L), cos_ref[...], sin_ref[...], s0_ref[...],
        dm_ref[...], qd_ref[...], kd_ref[...], sd_ref[...], gn_ref[...])
    o_ref[...] = y[:TS].astype(o_ref.dtype)
    so_ref[...] = s_new


def _ret_sample(h, cos, sin, consts, gn_w, state, layer, yr, stack):
    row0 = MP // TS

    def col_spec(off, width):
        return pl.BlockSpec((TS, width), lambda b, hd: (row0 + b, off // width + hd))

    table = pl.BlockSpec((SAMPLE_L, RET_DK // 2), lambda b, hd: (0, 0))
    state_spec = pl.BlockSpec((None, None, None, RET_DK, RET_DV), lambda b, hd: (layer, b, hd, 0, 0))
    stack_shape, extra_specs, extra_args = _state_stack(BS, stack)
    return pl.pallas_call(
        _ret_sample_kernel,
        out_shape=(jax.ShapeDtypeStruct(yr.shape, yr.dtype), stack_shape),
        grid=(BS, RET_HEADS),
        in_specs=[col_spec(QR, RET_DK), col_spec(KR, RET_DK), col_spec(VR, RET_DV), col_spec(GR, RET_DV),
                  table, table, *_head_const_specs(SAMPLE_L),
                  pl.BlockSpec((None, 1, RET_DV), lambda b, hd: (layer, 0, hd)),
                  state_spec, pl.BlockSpec(memory_space=pl.ANY), *extra_specs],
        out_specs=(pl.BlockSpec((TS, RET_DV), lambda b, hd: (row0 + b, hd)), state_spec),
        input_output_aliases={12: 0} if stack is None else {12: 0, 13: 1},
        compiler_params=_params(2),
        name="retention_sample",
    )(h, h, h, h, cos, sin, *consts, gn_w, state, yr, *extra_args)


def _conv_kernel(*refs, tt, aliased):
    if aliased:
        (ca_ref, cb_ref, hist_ref, w_ref, b_ref, lnw_ref, lnb_ref, _, o_ref, tail_ref,
         u_ref, y_ref, sh_ref) = refs
    else:
        (ca_ref, cb_ref, hist_ref, w_ref, b_ref, lnw_ref, lnb_ref, o_ref, tail_ref,
         u_ref, y_ref, sh_ref) = refs
    t = pl.program_id(1)

    @pl.when(t == 0)
    def _():
        u_ref[pl.ds(0, CONV_HIST), :] = hist_ref[...]

    u_ref[pl.ds(CONV_HIST, tt), :] = ca_ref[...].astype(F32) * _sigmoid(cb_ref[...].astype(F32))

    lanes = CONV_LANES
    n_win = 0
    for c0 in range(0, CONV_C, lanes):
        acc = jnp.zeros((tt, lanes), F32)
        for b in range(8):
            taps = [k for k in range(b, CONV_W, 8)]
            rows = tt + 8 * (len(taps) - 1)
            sh = sh_ref.at[n_win % 2]
            n_win += 1
            sh[pl.ds(0, rows), :] = u_ref[pl.ds(2 + b, rows), pl.ds(c0, lanes)]
            for a, k in enumerate(taps):
                acc = acc + sh[pl.ds(8 * a, tt), :] * w_ref[pl.ds(k, 1), pl.ds(c0, lanes)]
        y_ref[:, pl.ds(c0, lanes)] = acc

    y = y_ref[...] + b_ref[...]
    mu = jnp.mean(y, axis=-1, keepdims=True)
    var = jnp.mean(jnp.square(y - mu), axis=-1, keepdims=True)
    y = (y - mu) * lax.rsqrt(var + 1e-5) * lnw_ref[...] + lnb_ref[...]
    o_ref[...] = _silu(y).astype(o_ref.dtype)

    tail = u_ref[pl.ds(tt, CONV_HIST), :]
    tail_ref[...] = tail
    u_ref[pl.ds(0, CONV_HIST), :] = tail


def _conv(h, hist, dw_w, dw_b, ln_w, ln_b, layer, *, batch, steps, tt, row0, yc=None):
    def col_spec(off):
        return pl.BlockSpec((tt, CONV_C), lambda b, t: (row0 + b * steps + t, off // CONV_C))

    def vec_spec():
        return pl.BlockSpec((None, 1, CONV_C), lambda b, t: (layer, 0, 0))

    in_specs = [col_spec(CA), col_spec(CB),
                pl.BlockSpec((None, CONV_HIST, CONV_C), lambda b, t: (b, 0, 0)),
                pl.BlockSpec((None, 32, CONV_C), lambda b, t: (layer, 0, 0)),
                vec_spec(), vec_spec(), vec_spec()]
    args = [h, h, hist, dw_w, dw_b, ln_w, ln_b]
    aliases = {}
    if yc is not None:
        in_specs.append(pl.BlockSpec(memory_space=pl.ANY))
        args.append(yc)
        aliases = {7: 0}
    return pl.pallas_call(
        functools.partial(_conv_kernel, tt=tt, aliased=yc is not None),
        out_shape=(jax.ShapeDtypeStruct((M_ALL, CONV_C), BF16),
                   jax.ShapeDtypeStruct((batch, CONV_HIST, CONV_C), F32)),
        grid=(batch, steps),
        in_specs=in_specs,
        out_specs=(pl.BlockSpec((tt, CONV_C), lambda b, t: (row0 + b * steps + t, 0)),
                   pl.BlockSpec((None, CONV_HIST, CONV_C), lambda b, t: (b, 0, 0))),
        scratch_shapes=[pltpu.VMEM((CONV_HIST + tt, CONV_C), F32), pltpu.VMEM((tt, CONV_C), F32),
                        pltpu.VMEM((2, tt + 24, CONV_LANES), F32)],
        input_output_aliases=aliases,
        compiler_params=_params(2),
        name="conv_prompt" if yc is None else "conv_sample",
    )(*args)


def kernel(x_prompt, x_sample, cache_attn_k, cache_attn_v, state_ret, cache_conv, norm_mix_pre, norm_mix_post, norm_ffn_pre, norm_ffn_post, w_in, attn_rel_bias, ret_gn_w, conv_dw_w, conv_dw_b, conv_ln_w, conv_ln_b, w_branch_attn, w_branch_ret, w_branch_conv, w_out, w_ffn_gate, w_ffn_up, w_ffn_down):
    x = jnp.concatenate([x_prompt.reshape(MP, D_MODEL), x_sample.reshape(BS * TS, D_MODEL)], axis=0)

    def row_vec(w):
        return w.reshape(DEPTH, 1, w.shape[-1])

    n_mp, n_mq, n_fp, n_fq = map(row_vec, (norm_mix_pre, norm_mix_post, norm_ffn_pre, norm_ffn_post))
    gn_w, dw_b, ln_w, ln_b = map(row_vec, (ret_gn_w, conv_dw_b, conv_ln_w, conv_ln_b))
    dw_w = jnp.pad(conv_dw_w, ((0, 0), (0, 32 - CONV_W), (0, 0)))
    hist_p = jnp.zeros((BP, CONV_HIST, CONV_C), F32)
    hist_s = jnp.pad(cache_conv, ((0, 0), (0, 0), (CONV_HIST - (CONV_W - 1), 0), (0, 0)))

    bias_p, bias_s = _rel_bias_tiles(attn_rel_bias)
    cos_p, sin_p = _rope_tables(jnp.arange(TP))
    cos_s, sin_s = _rope_tables(PAST_LEN + jnp.arange(TS))
    cos_s = jnp.pad(cos_s, ((0, SAMPLE_L - TS), (0, 0)))
    sin_s = jnp.pad(sin_s, ((0, SAMPLE_L - TS), (0, 0)))
    ret_p = _retention_consts(RET_L, RET_L)
    ret_s = _retention_consts(SAMPLE_L, TS)

    kp_l, vp_l, cp_l, ks_l, vs_l, cs_l = ([] for _ in range(6))
    s_p = s_s = None
    xn = _pre_norm(x, n_mp, 0)
    for l in range(DEPTH):
        h = _proj(xn, w_in, l, tn=1024, tm=TM, name="in_proj")

        ya = _attn_prompt(h, bias_p, l)
        ya = _attn_sample(h, cache_attn_k, cache_attn_v, bias_s, l, ya)
        yr, s_p = _ret_prompt(h, cos_p, sin_p, ret_p, gn_w, l, s_p)
        yr, s_s = _ret_sample(h, cos_s, sin_s, ret_s, gn_w, state_ret, l, yr, s_s)
        yc, tail_p = _conv(h, hist_p, dw_w, dw_b, ln_w, ln_b, l,
                           batch=BP, steps=TP // CONV_TT, tt=CONV_TT, row0=0)
        yc, tail_s = _conv(h, hist_s[l], dw_w, dw_b, ln_w, ln_b, l,
                           batch=BS, steps=1, tt=TS, row0=MP // TS, yc=yc)

        merged = _merge(ya, yr, yc, h, w_branch_attn, w_branch_ret, w_branch_conv, l)
        y = _proj(merged, w_out, l, tn=1024, tm=TM, name="out_proj")
        x, xn = _post_norm(y, x, n_mq, l, n_fp, l)

        hidden = _swiglu_up(xn, w_ffn_gate, w_ffn_up, l)
        y = _proj(hidden, w_ffn_down, l, tn=512, tm=TM // 2, name="ffn_down")
        x, xn = _post_norm(y, x, n_fq, l, n_mp, min(l + 1, DEPTH - 1))

        kv_p = jnp.stack([h[(b + 1) * TP - ATT_BAND:(b + 1) * TP, KA:VA + ATT_WIDTH] for b in range(BP)])
        kv_s = h[MP:, KA:VA + ATT_WIDTH].reshape(BS, TS, 2 * ATT_WIDTH)
        kp_l.append(kv_p[:, :, :ATT_WIDTH].astype(F32).reshape(BP, ATT_BAND, ATT_HEADS, ATT_DH))
        vp_l.append(kv_p[:, :, ATT_WIDTH:].astype(F32).reshape(BP, ATT_BAND, ATT_HEADS, ATT_DH))
        ks_l.append(kv_s[:, :, :ATT_WIDTH].astype(F32).reshape(BS, TS, ATT_HEADS, ATT_DH))
        vs_l.append(kv_s[:, :, ATT_WIDTH:].astype(F32).reshape(BS, TS, ATT_HEADS, ATT_DH))
        cp_l.append(tail_p[:, CONV_HIST - (CONV_W - 1):])
        cs_l.append(tail_s[:, CONV_HIST - (CONV_W - 1):])

    return (x[:MP].reshape(BP, TP, D_MODEL), x[MP:].reshape(BS, TS, D_MODEL),
            jnp.stack(kp_l), jnp.stack(vp_l), s_p, jnp.stack(cp_l),
            jnp.stack(ks_l), jnp.stack(vs_l), s_s, jnp.stack(cs_l))
```
